```python
import jax, jax.numpy as jnp
from jax import lax
import numpy as np

D_MODEL = 1024
BATCH = 8
SEQ = 4096
DEPTH = 4

N_MIXERS = 4
EPS = 1e-6
FFN_HIDDEN = -(-8 * D_MODEL // (3 * 256)) * 256

POOL_WINDOWS = (2, 4, 8, 16)
POOL_GROUP = D_MODEL // len(POOL_WINDOWS)

SB_HEAD_DIM = 128
SB_HEADS = D_MODEL // SB_HEAD_DIM
SB_BLOCK = 128

RET_HEADS = max(4, D_MODEL // 256)
RET_QK_DIM = D_MODEL // RET_HEADS
RET_V_DIM = 2 * RET_QK_DIM
RET_CHUNK = 128
ROPE_BASE = 10000.0

GDN_K_DIM = 128
GDN_HEADS = D_MODEL // GDN_K_DIM
GDN_V_DIM = 2 * GDN_K_DIM
GDN_CONV = 4
GDN_CHUNK = 64

N_POOL = (DEPTH + 3) // 4
N_SB = (DEPTH + 2) // 4
N_RET = (DEPTH + 1) // 4
N_GDN = DEPTH // 4

kernel_name = "interleaved_pool_stickbreak_retnet_gdn_trunk"


def rms_norm(x, gain):
    xf = x.astype(jnp.float32)
    y = xf * lax.rsqrt(jnp.mean(xf * xf, axis=-1, keepdims=True) + EPS)
    return (y * gain.astype(jnp.float32)).astype(x.dtype)


def l2_norm(x):
    xf = x.astype(jnp.float32)
    return xf * lax.rsqrt(jnp.sum(xf * xf, axis=-1, keepdims=True) + EPS)


def to_chunks(t, c):
    b, s, h, d = t.shape
    return t.reshape(b, s // c, c, h, d).transpose(1, 0, 3, 2, 4)


def from_chunks(t):
    n, b, h, c, d = t.shape
    return t.transpose(1, 0, 3, 2, 4).reshape(b, n * c, h, d)


def rotary(x):
    s, d = x.shape[1], x.shape[-1]
    half = d // 2
    inv = ROPE_BASE ** (-jnp.arange(half, dtype=jnp.float32) / half)
    ang = jnp.arange(s, dtype=jnp.float32)[:, None] * inv[None, :]
    cos, sin = jnp.cos(ang)[None, :, None, :], jnp.sin(ang)[None, :, None, :]
    xf = x.astype(jnp.float32)
    x1, x2 = xf[..., :half], xf[..., half:]
    return jnp.concatenate([x1 * cos - x2 * sin, x1 * sin + x2 * cos], axis=-1)


def causal_depthwise_conv(x, w):
    k, c = w.shape
    return lax.conv_general_dilated(
        x, w[:, None, :].astype(x.dtype), window_strides=(1,), padding=[(k - 1, 0)],
        dimension_numbers=('NWC', 'WIO', 'NWC'), feature_group_count=c)


def pool_mixer(h, w_group, scale):
    b, s, d = h.shape
    hg = h.astype(jnp.float32).reshape(b, s, len(POOL_WINDOWS), POOL_GROUP)
    cs = jnp.cumsum(hg, axis=1)
    pos1 = jnp.arange(1, s + 1, dtype=jnp.float32)
    outs = []
    for g, w in enumerate(POOL_WINDOWS):
        c = cs[:, :, g]
        prev = jnp.pad(c, ((0, 0), (w, 0), (0, 0)))[:, :s]
        cnt = jnp.minimum(pos1, float(w))[None, :, None]
        outs.append((c - prev) / cnt - hg[:, :, g])
    pooled = jnp.stack(outs, axis=2).astype(h.dtype)
    mixed = jnp.einsum('bsgc,gcd->bsgd', pooled, w_group)
    return mixed.reshape(b, s, d) * scale


def stick_breaking_mixer(h, w_qkv, q_gain, k_gain, w_o):
    b, s, _ = h.shape
    qkv = (h @ w_qkv).reshape(b, s, 3, SB_HEADS, SB_HEAD_DIM)
    q = rms_norm(qkv[:, :, 0], q_gain).transpose(0, 2, 1, 3) * (SB_HEAD_DIM ** -0.5)
    k = rms_norm(qkv[:, :, 1], k_gain).transpose(0, 2, 1, 3)
    v = qkv[:, :, 2].transpose(0, 2, 1, 3)
    outs = []
    for blk in range(s // SB_BLOCK):
        q0, q1 = blk * SB_BLOCK, (blk + 1) * SB_BLOCK
        z = jnp.einsum('bhqd,bhkd->bhqk', q[:, :, q0:q1], k[:, :, :q1],
                       preferred_element_type=jnp.float32)
        t_idx = q0 + jnp.arange(SB_BLOCK)[:, None]
        s_idx = jnp.arange(q1)[None, :]
        causal = s_idx < t_idx
        log_1m_beta = jnp.where(causal, jax.nn.log_sigmoid(-z), 0.0)
        after = lax.cumsum(log_1m_beta, axis=3, reverse=True) - log_1m_beta
        wts = jnp.where(causal, jnp.exp(jax.nn.log_sigmoid(z) + after), 0.0)
        outs.append(jnp.einsum('bhqk,bhkd->bhqd', wts.astype(v.dtype), v[:, :, :q1]))
    o = jnp.concatenate(outs, axis=2)
    return o.transpose(0, 2, 1, 3).reshape(b, s, SB_HEADS * SB_HEAD_DIM) @ w_o


def retention_mixer(h, w_in, gn_gain, w_o):
    b, s, _ = h.shape
    H, dk, dv, C = RET_HEADS, RET_QK_DIM, RET_V_DIM, RET_CHUNK
    proj = h @ w_in
    q, k, v, g = jnp.split(proj, [H * dk, 2 * H * dk, 2 * H * dk + H * dv], axis=-1)
    q = rotary(q.reshape(b, s, H, dk))
    k = rotary(k.reshape(b, s, H, dk)) * (dk ** -0.5)
    v = v.reshape(b, s, H, dv).astype(jnp.float32)
    log_gamma = jnp.log1p(-jnp.exp2(-5.0 - jnp.arange(H, dtype=jnp.float32)))
    idx = jnp.arange(C, dtype=jnp.float32)
    diff = idx[:, None] - idx[None, :]
    intra = jnp.where(diff >= 0, jnp.exp(jnp.maximum(diff, 0.0)[None] * log_gamma[:, None, None]), 0.0)
    q_dec = jnp.exp((idx + 1.0)[None] * log_gamma[:, None])[..., None]
    k_dec = jnp.exp((C - 1.0 - idx)[None] * log_gamma[:, None])[..., None]
    chunk_dec = jnp.exp(C * log_gamma)[:, None, None]

    def step(state, inp):
        qn, kn, vn = inp
        scores = jnp.einsum('bhcd,bhmd->bhcm', qn, kn) * intra
        o = (jnp.einsum('bhcm,bhme->bhce', scores, vn)
             + jnp.einsum('bhcd,bhde->bhce', qn * q_dec, state))
        state = state * chunk_dec + jnp.einsum('bhcd,bhce->bhde', kn * k_dec, vn)
        return state, o

    s0 = jnp.zeros((b, H, dk, dv), jnp.float32)
    _, o = lax.scan(step, s0, (to_chunks(q, C), to_chunks(k, C), to_chunks(v, C)))
    o = rms_norm(from_chunks(o), gn_gain).reshape(b, s, H * dv)
    o = o.astype(h.dtype) * jax.nn.silu(g)
    return o @ w_o


def gdn_mixer(h, w_in, conv_w, a_log, dt_bias, norm_gain, w_o):
    b, s, _ = h.shape
    H, dk, dv, C = GDN_HEADS, GDN_K_DIM, GDN_V_DIM, GDN_CHUNK
    n_qkv = 2 * H * dk + H * dv
    proj = h @ w_in
    qkv, z, a, bt = jnp.split(proj, [n_qkv, n_qkv + H * dv, n_qkv + H * dv + H], axis=-1)
    qkv = jax.nn.silu(causal_depthwise_conv(qkv, conv_w))
    q, k, v = jnp.split(qkv, [H * dk, 2 * H * dk], axis=-1)
    q = l2_norm(q.reshape(b, s, H, dk)) * (dk ** -0.5)
    k = l2_norm(k.reshape(b, s, H, dk))
    v = v.reshape(b, s, H, dv).astype(jnp.float32)
    beta = jax.nn.sigmoid(bt.astype(jnp.float32))
    g = -jnp.exp(a_log) * jax.nn.softplus(a.astype(jnp.float32) + dt_bias)

    tril = jnp.tril(jnp.ones((C, C), dtype=bool))
    strict = jnp.tril(jnp.ones((C, C), dtype=bool), -1)
    eye = jnp.eye(C, dtype=jnp.float32)

    def step(state, inp):
        qn, kn, vn, bn, gn = inp
        G = jnp.cumsum(gn, axis=-1)
        diff = G[..., :, None] - G[..., None, :]
        decay = jnp.where(tril, jnp.exp(jnp.where(tril, diff, 0.0)), 0.0)
        kb = kn * bn[..., None]
        n_mat = jnp.where(strict, jnp.einsum('bhcd,bhmd->bhcm', kb, kn) * decay, 0.0)
        t_mat = n_mat + eye
        u = lax.linalg.triangular_solve(t_mat, vn * bn[..., None], left_side=True,
                                        lower=True, unit_diagonal=True)
        w = lax.linalg.triangular_solve(t_mat, kb * jnp.exp(G)[..., None], left_side=True,
                                        lower=True, unit_diagonal=True)
        v_new = u - jnp.einsum('bhcd,bhde->bhce', w, state)
        attn = jnp.where(tril, jnp.einsum('bhcd,bhmd->bhcm', qn, kn) * decay, 0.0)
        o = (jnp.einsum('bhcd,bhde->bhce', qn * jnp.exp(G)[..., None], state)
             + jnp.einsum('bhcm,bhme->bhce', attn, v_new))
        g_last = G[..., -1:]
        state = (state * jnp.exp(g_last)[..., None]
                 + jnp.einsum('bhcd,bhce->bhde', kn * jnp.exp(g_last - G)[..., None], v_new))
        return state, o

    s0 = jnp.zeros((b, H, dk, dv), jnp.float32)
    xs = (to_chunks(q, C), to_chunks(k, C), to_chunks(v, C),
          to_chunks(beta[..., None], C)[..., 0], to_chunks(g[..., None], C)[..., 0])
    _, o = lax.scan(step, s0, xs)
    o = rms_norm(from_chunks(o), norm_gain).reshape(b, s, H * dv)
    o = o.astype(h.dtype) * jax.nn.silu(z)
    return o @ w_o


def swiglu(h, w_in, w_out):
    gate, up = jnp.split(h @ w_in, 2, axis=-1)
    return (jax.nn.silu(gate) * up) @ w_out


def setup_inputs(seed: int = 0) -> dict:
    key = jax.random.key(seed)
    ks = iter(jax.random.split(key, 32))

    def nrm(shape, fan_in):
        return jax.random.normal(next(ks), shape, jnp.float32) * (fan_in ** -0.5)

    def gain(shape):
        return 1.0 + 0.02 * jax.random.normal(next(ks), shape, jnp.float32)

    D, F = D_MODEL, FFN_HIDDEN
    x = jax.random.normal(next(ks), (BATCH, SEQ, D), jnp.float32)
    norm_mix = gain((DEPTH, D))
    norm_ffn = gain((DEPTH, D))
    ffn_w_in = nrm((DEPTH, D, 2 * F), D)
    ffn_w_out = nrm((DEPTH, F, D), F)
    pool_w = nrm((N_POOL, len(POOL_WINDOWS), POOL_GROUP, POOL_GROUP), POOL_GROUP)
    pool_scale = gain((N_POOL, D))
    sb_w_qkv = nrm((N_SB, D, 3 * SB_HEADS * SB_HEAD_DIM), D)
    sb_q_gain = gain((N_SB, SB_HEAD_DIM))
    sb_k_gain = gain((N_SB, SB_HEAD_DIM))
    sb_w_o = nrm((N_SB, SB_HEADS * SB_HEAD_DIM, D), SB_HEADS * SB_HEAD_DIM)
    ret_w_in = nrm((N_RET, D, 2 * RET_HEADS * RET_QK_DIM + 2 * RET_HEADS * RET_V_DIM), D)
    ret_gn_gain = gain((N_RET, RET_V_DIM))
    ret_w_o = nrm((N_RET, RET_HEADS * RET_V_DIM, D), RET_HEADS * RET_V_DIM)
    gdn_w_in = nrm((N_GDN, D, 2 * GDN_HEADS * GDN_K_DIM + 2 * GDN_HEADS * GDN_V_DIM + 2 * GDN_HEADS), D)
    gdn_conv_w = nrm((N_GDN, GDN_CONV, 2 * GDN_HEADS * GDN_K_DIM + GDN_HEADS * GDN_V_DIM), GDN_CONV)
    gdn_a_log = jnp.log(jax.random.uniform(next(ks), (N_GDN, GDN_HEADS), jnp.float32, 1.0, 16.0))
    dt = jnp.exp(jax.random.uniform(next(ks), (N_GDN, GDN_HEADS), jnp.float32,
                                    float(np.log(1e-3)), float(np.log(1e-1))))
    gdn_dt_bias = dt + jnp.log(-jnp.expm1(-dt))
    gdn_norm_gain = gain((N_GDN, GDN_V_DIM))
    gdn_w_o = nrm((N_GDN, GDN_HEADS * GDN_V_DIM, D), GDN_HEADS * GDN_V_DIM)
    return {"x": x, "norm_mix": norm_mix, "norm_ffn": norm_ffn,
            "ffn_w_in": ffn_w_in, "ffn_w_out": ffn_w_out,
            "pool_w": pool_w, "pool_scale": pool_scale,
            "sb_w_qkv": sb_w_qkv, "sb_q_gain": sb_q_gain, "sb_k_gain": sb_k_gain, "sb_w_o": sb_w_o,
            "ret_w_in": ret_w_in, "ret_gn_gain": ret_gn_gain, "ret_w_o": ret_w_o,
            "gdn_w_in": gdn_w_in, "gdn_conv_w": gdn_conv_w, "gdn_a_log": gdn_a_log,
            "gdn_dt_bias": gdn_dt_bias, "gdn_norm_gain": gdn_norm_gain, "gdn_w_o": gdn_w_o}


def reference(x, norm_mix, norm_ffn, ffn_w_in, ffn_w_out, pool_w, pool_scale,
              sb_w_qkv, sb_q_gain, sb_k_gain, sb_w_o, ret_w_in, ret_gn_gain, ret_w_o,
              gdn_w_in, gdn_conv_w, gdn_a_log, gdn_dt_bias, gdn_norm_gain, gdn_w_o):
    for layer in range(DEPTH):
        mixer, occ = layer % N_MIXERS, layer // N_MIXERS
        h = rms_norm(x, norm_mix[layer])
        if mixer == 0:
            y = pool_mixer(h, pool_w[occ], pool_scale[occ])
        elif mixer == 1:
            y = stick_breaking_mixer(h, sb_w_qkv[occ], sb_q_gain[occ], sb_k_gain[occ], sb_w_o[occ])
        elif mixer == 2:
            y = retention_mixer(h, ret_w_in[occ], ret_gn_gain[occ], ret_w_o[occ])
        else:
            y = gdn_mixer(h, gdn_w_in[occ], gdn_conv_w[occ], gdn_a_log[occ], gdn_dt_bias[occ],
                          gdn_norm_gain[occ], gdn_w_o[occ])
        x = x + y.astype(x.dtype)
        x = x + swiglu(rms_norm(x, norm_ffn[layer]), ffn_w_in[layer], ffn_w_out[layer]).astype(x.dtype)
    return x
```

```python
import functools

import jax
import jax.numpy as jnp
from jax import lax
from jax.experimental import pallas as pl
from jax.experimental.pallas import tpu as pltpu

F32 = jnp.float32
BF16 = jnp.bfloat16

EPS = 1e-6
ROPE_BASE = 10000.0
LANES = 128
SUBLANES = 8
VMEM_LIMIT = 56 * 1024 * 1024

POOL_WINDOWS = (2, 4, 8, 16)
POOL_HALO = 32
SB_HEAD_DIM = 128
RET_HEADS = 4
RET_CHUNK = 128
GDN_K_DIM = 128
GDN_CONV = 4
GDN_CHUNK = 64


def _params(*sem):
    return pltpu.CompilerParams(dimension_semantics=sem, vmem_limit_bytes=VMEM_LIMIT)


def _rms(xf, gain_row):
    ms = jnp.mean(xf * xf, axis=-1, keepdims=True)
    return xf * lax.rsqrt(ms + EPS) * gain_row


def _silu(x):
    return x * jax.nn.sigmoid(x)


def _dot(a, b):
    return jnp.dot(a, b, preferred_element_type=F32)


def _dot_nt(a, b):
    return lax.dot_general(a, b, (((1,), (1,)), ((), ())), preferred_element_type=F32)


def _dot_tn(a, b):
    return lax.dot_general(a, b, (((0,), (0,)), ((), ())), preferred_element_type=F32)


def _split_dot_rhs01(a_f32, b01):
    hi = a_f32.astype(BF16)
    lo = (a_f32 - hi.astype(F32)).astype(BF16)
    return _dot(hi, b01) + _dot(lo, b01)


def _split_dot_lhs01(a01, b_f32):
    hi = b_f32.astype(BF16)
    lo = (b_f32 - hi.astype(F32)).astype(BF16)
    return _dot(a01, hi) + _dot(a01, lo)


def _ffn_kernel(x_ref, g_ref, wg_ref, wu_ref, wo_ref, o_ref, h_scr, acc_scr):
    f = pl.program_id(1)

    @pl.when(f == 0)
    def _():
        h_scr[...] = _rms(x_ref[...], g_ref[...]).astype(BF16)
        acc_scr[...] = jnp.zeros_like(acc_scr)

    h = h_scr[...]
    gate = _dot(h, wg_ref[...])
    up = _dot(h, wu_ref[...])
    act = (_silu(gate) * up).astype(BF16)
    acc_scr[...] += _dot(act, wo_ref[...])

    @pl.when(f == pl.num_programs(1) - 1)
    def _():
        o_ref[...] = x_ref[...] + acc_scr[...]


def _ffn(x2, gain, w_in, w_out, *, tm=512, tf=1408):
    t, d = x2.shape
    f_hidden = w_out.shape[0]
    nf = f_hidden // tf
    assert t % tm == 0 and nf * tf == f_hidden
    return pl.pallas_call(
        _ffn_kernel,
        grid=(t // tm, nf),
        in_specs=[
            pl.BlockSpec((tm, d), lambda i, f: (i, 0)),
            pl.BlockSpec((1, d), lambda i, f: (0, 0)),
            pl.BlockSpec((d, tf), lambda i, f: (0, f)),
            pl.BlockSpec((d, tf), lambda i, f: (0, f + nf)),
            pl.BlockSpec((tf, d), lambda i, f: (f, 0)),
        ],
        out_specs=pl.BlockSpec((tm, d), lambda i, f: (i, 0)),
        out_shape=jax.ShapeDtypeStruct((t, d), F32),
        scratch_shapes=[pltpu.VMEM((tm, d), BF16), pltpu.VMEM((tm, d), F32)],
        compiler_params=_params("parallel", "arbitrary"),
        name="ffn",
    )(x2, gain.reshape(1, d), w_in, w_in, w_out)


def _oproj_kernel(a_ref, w_ref, x_ref, o_ref):
    o_ref[...] = x_ref[...] + _dot(a_ref[...], w_ref[...])


def _oproj(a2, w, x2, *, tm=512):
    t, k = a2.shape
    d = w.shape[1]
    assert t % tm == 0
    return pl.pallas_call(
        _oproj_kernel,
        grid=(t // tm,),
        in_specs=[
            pl.BlockSpec((tm, k), lambda i: (i, 0)),
            pl.BlockSpec((k, d), lambda i: (0, 0)),
            pl.BlockSpec((tm, d), lambda i: (i, 0)),
        ],
        out_specs=pl.BlockSpec((tm, d), lambda i: (i, 0)),
        out_shape=jax.ShapeDtypeStruct((t, d), F32),
        compiler_params=_params("parallel"),
        name="oproj",
    )(a2, w, x2)


def _pool_kernel(x_ref, g_ref, w_ref, sc_ref, o_ref, ext, buf_a, buf_b, *, ts, group):
    s = pl.program_id(1)
    n = ts + POOL_HALO
    d = x_ref.shape[-1]

    @pl.when(s == 0)
    def _():
        ext[0:POOL_HALO, :] = jnp.zeros((POOL_HALO, d), F32)

    x = x_ref[...]
    ext[POOL_HALO:n, :] = _rms(x, g_ref[...])

    src, dst = ext, buf_a
    for level in range(1, len(POOL_WINDOWS) + 1):
        shift = 1 << (level - 1)
        lo = SUBLANES * level
        c0 = (level - 1) * group
        dst[lo:n, c0:d] = src[lo:n, c0:d] + src[lo - shift:n - shift, c0:d]
        src, dst = dst, (buf_b if dst is buf_a else buf_a)

    pos1 = (s * ts + lax.broadcasted_iota(jnp.int32, (ts, 1), 0) + 1).astype(F32)
    for g, win in enumerate(POOL_WINDOWS):
        level_buf = buf_a if g % 2 == 0 else buf_b
        cols = slice(g * group, (g + 1) * group)
        inv_cnt = 1.0 / jnp.minimum(pos1, float(win))
        pooled = level_buf[POOL_HALO:n, cols] * inv_cnt - ext[POOL_HALO:n, cols]
        mixed = _dot(pooled.astype(BF16), w_ref[g])
        o_ref[:, cols] = x[:, cols] + mixed * sc_ref[:, cols]

    ext[0:POOL_HALO, :] = ext[ts:n, :]


def _pool_layer(x, gain, w_group, scale, *, ts=512):
    b, s, d = x.shape
    ngroups, group, _ = w_group.shape
    assert s % ts == 0 and ngroups == len(POOL_WINDOWS) and ngroups * group == d
    n = ts + POOL_HALO
    return pl.pallas_call(
        functools.partial(_pool_kernel, ts=ts, group=group),
        grid=(b, s // ts),
        in_specs=[
            pl.BlockSpec((None, ts, d), lambda i, j: (i, j, 0)),
            pl.BlockSpec((1, d), lambda i, j: (0, 0)),
            pl.BlockSpec((ngroups, group, group), lambda i, j: (0, 0, 0)),
            pl.BlockSpec((1, d), lambda i, j: (0, 0)),
        ],
        out_specs=pl.BlockSpec((None, ts, d), lambda i, j: (i, j, 0)),
        out_shape=jax.ShapeDtypeStruct((b, s, d), F32),
        scratch_shapes=[pltpu.VMEM((n, d), F32)] * 3,
        compiler_params=_params("arbitrary", "arbitrary"),
        name="pool_layer",
    )(x, gain.reshape(1, d), w_group, scale.reshape(1, d))


def _sb_proj_kernel(x_ref, g_ref, w_ref, qg_ref, kg_ref, o_ref, h_scr):
    j = pl.program_id(1)

    @pl.when(j == 0)
    def _():
        h_scr[...] = _rms(x_ref[...], g_ref[...]).astype(BF16)

    y = _dot(h_scr[...], w_ref[...])

    @pl.when(j < 2)
    def _():
        gain = jnp.where(j == 0, qg_ref[...] * (SB_HEAD_DIM ** -0.5), kg_ref[...])
        for c in range(y.shape[1] // SB_HEAD_DIM):
            cols = slice(c * SB_HEAD_DIM, (c + 1) * SB_HEAD_DIM)
            o_ref[:, cols] = _rms(y[:, cols], gain).astype(o_ref.dtype)

    @pl.when(j == 2)
    def _():
        o_ref[...] = y.astype(o_ref.dtype)


def _sb_proj(x2, gain, w_qkv, q_gain, k_gain, *, tm=512):
    t, d = x2.shape
    n = w_qkv.shape[1]
    tn = n // 3
    assert t % tm == 0 and tn % SB_HEAD_DIM == 0
    return pl.pallas_call(
        _sb_proj_kernel,
        grid=(t // tm, 3),
        in_specs=[
            pl.BlockSpec((tm, d), lambda i, j: (i, 0)),
            pl.BlockSpec((1, d), lambda i, j: (0, 0)),
            pl.BlockSpec((d, tn), lambda i, j: (0, j)),
            pl.BlockSpec((1, SB_HEAD_DIM), lambda i, j: (0, 0)),
            pl.BlockSpec((1, SB_HEAD_DIM), lambda i, j: (0, 0)),
        ],
        out_specs=pl.BlockSpec((tm, tn), lambda i, j: (i, j)),
        out_shape=jax.ShapeDtypeStruct((t, n), BF16),
        scratch_shapes=[pltpu.VMEM((tm, d), BF16)],
        compiler_params=_params("parallel", "arbitrary"),
        name="sb_proj",
    )(x2, gain.reshape(1, d), w_qkv, q_gain.reshape(1, -1), k_gain.reshape(1, -1))


def _sb_attn_kernel(q_ref, k_ref, v_ref, o_ref, *, tq, tk):
    i = pl.program_id(2)
    q = q_ref[...]
    row = i * tq + lax.broadcasted_iota(jnp.int32, (tq, tk), 0)
    col = lax.broadcasted_iota(jnp.int32, (tq, tk), 1)
    jj = lax.broadcasted_iota(jnp.int32, (tk, 2 * tk), 0)
    ss = lax.broadcasted_iota(jnp.int32, (tk, 2 * tk), 1)
    suffix = jnp.where((jj > ss) | (ss >= tk), 1.0, 0.0).astype(BF16)

    def block(kb, acc, run, masked):
        start = pl.multiple_of(kb * tk, tk)
        k = k_ref[pl.ds(start, tk), :]
        v = v_ref[pl.ds(start, tk), :]
        z = _dot_nt(q, k)
        lse = jnp.log(1.0 + jnp.exp(-jnp.abs(z)))
        log_1m_beta = -jnp.maximum(z, 0.0) - lse
        if masked:
            causal = (start + col) < row
            log_1m_beta = jnp.where(causal, log_1m_beta, 0.0)
        sums = _split_dot_rhs01(log_1m_beta, suffix)
        after = sums[:, :tk] + run
        wts = jnp.exp(jnp.minimum(z, 0.0) - lse + after)
        if masked:
            wts = jnp.where(causal, wts, 0.0)
        acc = acc + _dot(wts.astype(BF16), v)
        run = run + sums[:, tk:]
        return acc, run

    acc = jnp.zeros((tq, q.shape[1]), F32)
    run = jnp.zeros((tq, tk), F32)
    n_diag = tq // tk
    first_diag = i * n_diag
    for m in range(n_diag - 1, -1, -1):
        acc, run = block(first_diag + m, acc, run, True)

    def body(t, carry):
        return block(first_diag - 1 - t, carry[0], carry[1], False)

    acc, run = lax.fori_loop(0, first_diag, body, (acc, run))
    o_ref[...] = acc.astype(o_ref.dtype)


def _sb_attn(qkv, *, heads, tq=128, tk=128):
    b, s, _ = qkv.shape
    dh = SB_HEAD_DIM
    assert s % tq == 0 and tq % tk == 0
    return pl.pallas_call(
        functools.partial(_sb_attn_kernel, tq=tq, tk=tk),
        grid=(b, heads, s // tq),
        in_specs=[
            pl.BlockSpec((None, tq, dh), lambda bi, h, i: (bi, i, h)),
            pl.BlockSpec((None, s, dh), lambda bi, h, i: (bi, 0, heads + h)),
            pl.BlockSpec((None, s, dh), lambda bi, h, i: (bi, 0, 2 * heads + h)),
        ],
        out_specs=pl.BlockSpec((None, tq, dh), lambda bi, h, i: (bi, i, h)),
        out_shape=jax.ShapeDtypeStruct((b, s, heads * dh), BF16),
        compiler_params=_params("parallel", "parallel", "arbitrary"),
        name="sb_attn",
    )(qkv, qkv, qkv)


def _ret_proj_kernel(x_ref, g_ref, w_ref, cos_ref, sin_ref, o_ref, h_scr, *, dk):
    j = pl.program_id(1)

    @pl.when(j == 0)
    def _():
        h_scr[...] = _rms(x_ref[...], g_ref[...]).astype(BF16)

    y = _dot(h_scr[...], w_ref[...])
    half = dk // 2

    @pl.when(j < 2)
    def _():
        scale = jnp.where(j == 0, 1.0, dk ** -0.5)
        cos = cos_ref[...] * scale
        sin = sin_ref[...] * scale
        for c in range(y.shape[1] // dk):
            x1 = y[:, c * dk:c * dk + half]
            x2 = y[:, c * dk + half:(c + 1) * dk]
            o_ref[:, c * dk:c * dk + half] = (x1 * cos - x2 * sin).astype(o_ref.dtype)
            o_ref[:, c * dk + half:(c + 1) * dk] = (x1 * sin + x2 * cos).astype(o_ref.dtype)

    @pl.when(j >= 2)
    def _():
        o_ref[...] = y.astype(o_ref.dtype)


def _ret_proj(x2, gain, w_in, cos, sin, *, seq, dk, tm=512, tn=1024):
    t, d = x2.shape
    n = w_in.shape[1]
    assert t % tm == 0 and seq % tm == 0 and n % tn == 0 and tn % dk == 0
    assert (RET_HEADS * dk) % tn == 0
    per_seq = seq // tm
    return pl.pallas_call(
        functools.partial(_ret_proj_kernel, dk=dk),
        grid=(t // tm, n // tn),
        in_specs=[
            pl.BlockSpec((tm, d), lambda i, j: (i, 0)),
            pl.BlockSpec((1, d), lambda i, j: (0, 0)),
            pl.BlockSpec((d, tn), lambda i, j: (0, j)),
            pl.BlockSpec((tm, dk // 2), lambda i, j: (i % per_seq, 0)),
            pl.BlockSpec((tm, dk // 2), lambda i, j: (i % per_seq, 0)),
        ],
        out_specs=pl.BlockSpec((tm, tn), lambda i, j: (i, j)),
        out_shape=jax.ShapeDtypeStruct((t, n), BF16),
        scratch_shapes=[pltpu.VMEM((tm, d), BF16)],
        compiler_params=_params("parallel", "arbitrary"),
        name="ret_proj",
    )(x2, gain.reshape(1, d), w_in, cos, sin)


def _ret_kernel(lg_ref, q_ref, k_ref, v_ref, g_ref, gn_ref, o_ref, state, *, rows, chunk):
    c = pl.program_id(2)

    @pl.when(c == 0)
    def _():
        state[...] = jnp.zeros_like(state)

    lg = lg_ref[...][:, :1]
    ri = lax.broadcasted_iota(jnp.int32, (chunk, chunk), 0)
    ci = lax.broadcasted_iota(jnp.int32, (chunk, chunk), 1)
    diff = (ri - ci).astype(F32)
    intra = jnp.where(diff >= 0, jnp.exp(jnp.maximum(diff, 0.0) * lg), 0.0)
    idx = lax.broadcasted_iota(jnp.int32, (chunk, 1), 0).astype(F32)
    q_dec = jnp.exp((idx + 1.0) * lg)
    k_dec = jnp.exp((chunk - 1.0 - idx) * lg)
    chunk_dec = jnp.exp(chunk * lg)

    for n in range(rows // chunk):
        r = slice(n * chunk, (n + 1) * chunk)
        q = q_ref[r, :]
        k = k_ref[r, :]
        v = v_ref[r, :]
        st = state[...]
        scores = _dot_nt(q, k) * intra
        o = _dot(scores.astype(BF16), v) + q_dec * _dot(q, st.astype(BF16))
        kd = (k.astype(F32) * k_dec).astype(BF16)
        state[...] = st * chunk_dec + _dot_tn(kd, v)
        gate = g_ref[r, :].astype(F32)
        o_ref[r, :] = (_rms(o, gn_ref[...]) * _silu(gate)).astype(o_ref.dtype)


def _ret_core(proj, lg_rows, gn_gain, *, heads, dk, dv, rows=512):
    b, s, _ = proj.shape
    assert s % rows == 0 and rows % RET_CHUNK == 0 and dv == 2 * dk
    return pl.pallas_call(
        functools.partial(_ret_kernel, rows=rows, chunk=RET_CHUNK),
        grid=(b, heads, s // rows),
        in_specs=[
            pl.BlockSpec((None, 1, LANES), lambda bi, h, c: (h, 0, 0)),
            pl.BlockSpec((None, rows, dk), lambda bi, h, c: (bi, c, h)),
            pl.BlockSpec((None, rows, dk), lambda bi, h, c: (bi, c, heads + h)),
            pl.BlockSpec((None, rows, dv), lambda bi, h, c: (bi, c, heads + h)),
            pl.BlockSpec((None, rows, dv), lambda bi, h, c: (bi, c, 2 * heads + h)),
            pl.BlockSpec((1, dv), lambda bi, h, c: (0, 0)),
        ],
        out_specs=pl.BlockSpec((None, rows, dv), lambda bi, h, c: (bi, c, h)),
        out_shape=jax.ShapeDtypeStruct((b, s, heads * dv), BF16),
        scratch_shapes=[pltpu.VMEM((dk, dv), F32)],
        compiler_params=_params("parallel", "parallel", "arbitrary"),
        name="ret_core",
    )(lg_rows, proj, proj, proj, proj, gn_gain.reshape(1, dv))


def _gdn_proj_kernel(x_ref, g_ref, w_ref, cw_ref, o_ref, h_scr, ybuf, halo, *, per_seq, n_conv, dk):
    i = pl.program_id(0)
    j = pl.program_id(1)
    tm = x_ref.shape[0]

    @pl.when(j == 0)
    def _():
        h_scr[...] = _rms(x_ref[...], g_ref[...]).astype(BF16)

    y = _dot(h_scr[...], w_ref[...])

    @pl.when(j < n_conv)
    def _():
        jc = jnp.minimum(j, n_conv - 1)

        @pl.when(i % per_seq == 0)
        def _():
            ybuf[0:SUBLANES, :] = jnp.zeros((SUBLANES, y.shape[1]), F32)

        @pl.when(i % per_seq != 0)
        def _():
            ybuf[0:SUBLANES, :] = halo[jc]

        ybuf[SUBLANES:SUBLANES + tm, :] = y
        halo[jc] = y[tm - SUBLANES:tm, :]
        conv = jnp.zeros_like(y)
        for tap in range(GDN_CONV):
            off = SUBLANES - (GDN_CONV - 1) + tap
            conv = conv + cw_ref[tap:tap + 1, :] * ybuf[off:off + tm, :]
        act = _silu(conv)

        @pl.when(j < 2)
        def _():
            scale = jnp.where(j == 0, dk ** -0.5, 1.0)
            for c in range(act.shape[1] // dk):
                cols = slice(c * dk, (c + 1) * dk)
                a = act[:, cols]
                nrm = lax.rsqrt(jnp.sum(a * a, axis=-1, keepdims=True) + EPS) * scale
                o_ref[:, cols] = (a * nrm).astype(o_ref.dtype)

        @pl.when(j >= 2)
        def _():
            o_ref[...] = act.astype(o_ref.dtype)

    @pl.when(j >= n_conv)
    def _():
        o_ref[...] = y.astype(o_ref.dtype)


def _gdn_proj(x2, gain, w_main, conv_w, *, seq, heads, dk, tm=512, tn=1024):
    t, d = x2.shape
    n = w_main.shape[1]
    n_conv = conv_w.shape[1] // tn
    assert t % tm == 0 and seq % tm == 0 and n % tn == 0 and conv_w.shape[1] % tn == 0
    assert heads * dk == tn
    per_seq = seq // tm
    return pl.pallas_call(
        functools.partial(_gdn_proj_kernel, per_seq=per_seq, n_conv=n_conv, dk=dk),
        grid=(t // tm, n // tn),
        in_specs=[
            pl.BlockSpec((tm, d), lambda i, j: (i, 0)),
            pl.BlockSpec((1, d), lambda i, j: (0, 0)),
            pl.BlockSpec((d, tn), lambda i, j: (0, j)),
            pl.BlockSpec((GDN_CONV, tn), lambda i, j: (0, jnp.minimum(j, n_conv - 1))),
        ],
        out_specs=pl.BlockSpec((tm, tn), lambda i, j: (i, j)),
        out_shape=jax.ShapeDtypeStruct((t, n), BF16),
        scratch_shapes=[
            pltpu.VMEM((tm, d), BF16),
            pltpu.VMEM((tm + SUBLANES, tn), F32),
            pltpu.VMEM((n_conv, SUBLANES, tn), F32),
        ],
        compiler_params=_params("arbitrary", "arbitrary"),
        name="gdn_proj",
    )(x2, gain.reshape(1, d), w_main, conv_w)


def _gdn_gate_kernel(x_ref, g_ref, w_ref, alog_ref, dtb_ref, o_ref, *, heads):
    h = _rms(x_ref[...], g_ref[...]).astype(BF16)
    y = _dot(h, w_ref[...])
    a = y + dtb_ref[...]
    softplus = jnp.maximum(a, 0.0) + jnp.log(1.0 + jnp.exp(-jnp.abs(a)))
    log_decay = -jnp.exp(alog_ref[...]) * softplus
    beta = jax.nn.sigmoid(y)
    lane = lax.broadcasted_iota(jnp.int32, y.shape, 1)
    o_ref[...] = jnp.where(lane < heads, log_decay, beta)


def _gdn_gate(x2, gain, w_ab, a_log_row, dt_bias_row, *, heads, tm=1024):
    t, d = x2.shape
    assert t % tm == 0
    return pl.pallas_call(
        functools.partial(_gdn_gate_kernel, heads=heads),
        grid=(t // tm,),
        in_specs=[
            pl.BlockSpec((tm, d), lambda i: (i, 0)),
            pl.BlockSpec((1, d), lambda i: (0, 0)),
            pl.BlockSpec((d, LANES), lambda i: (0, 0)),
            pl.BlockSpec((1, LANES), lambda i: (0, 0)),
            pl.BlockSpec((1, LANES), lambda i: (0, 0)),
        ],
        out_specs=pl.BlockSpec((tm, LANES), lambda i: (i, 0)),
        out_shape=jax.ShapeDtypeStruct((t, LANES), F32),
        compiler_params=_params("parallel"),
        name="gdn_gate",
    )(x2, gain.reshape(1, d), w_ab, a_log_row, dt_bias_row)


def _gdn_kernel(q_ref, k_ref, v_ref, z_ref, bg_ref, gain_ref, o_ref, state, *, rows, chunk, heads):
    head = pl.program_id(1)
    c = pl.program_id(2)
    C = chunk

    @pl.when(c == 0)
    def _():
        state[...] = jnp.zeros_like(state)

    lane = lax.broadcasted_iota(jnp.int32, (C, LANES), 1)
    ri = lax.broadcasted_iota(jnp.int32, (C, C), 0)
    ci = lax.broadcasted_iota(jnp.int32, (C, C), 1)
    tril = ci <= ri
    strict = ci < ri
    eye = jnp.where(ci == ri, 1.0, 0.0)
    ai = lax.broadcasted_iota(jnp.int32, (2 * C, C), 0)
    am = lax.broadcasted_iota(jnp.int32, (2 * C, C), 1)
    sum_sel = jnp.where(((ai < C) & (am <= ai)) | ((ai >= C) & (am > ai - C)), 1.0, 0.0).astype(BF16)
    bm = lax.broadcasted_iota(jnp.int32, (C, 2 * C), 0)
    bj = lax.broadcasted_iota(jnp.int32, (C, 2 * C), 1)
    col_sel = jnp.where((bm > bj) | (bj >= C), 1.0, 0.0)

    for n in range(rows // C):
        r = slice(n * C, (n + 1) * C)
        bg = bg_ref[r, :]
        g_col = jnp.sum(jnp.where(lane == head, bg, 0.0), axis=-1, keepdims=True)
        b_col = jnp.sum(jnp.where(lane == head + heads, bg, 0.0), axis=-1, keepdims=True)
        sums = _split_dot_lhs01(sum_sel, g_col * col_sel)
        diff = sums[:C, :C]
        g_cum = sums[:C, C:C + 1]
        g_rem = sums[C:, C:C + 1]
        g_last = g_cum[C - 1:C, :]
        decay = jnp.where(tril, jnp.exp(diff), 0.0)
        exp_g = jnp.exp(g_cum)

        q = q_ref[r, :]
        k = k_ref[r, :]
        kf = k.astype(F32)
        k_beta = kf * b_col
        n_mat = jnp.where(strict, _dot_nt(k_beta.astype(BF16), k) * decay, 0.0)
        inv = eye - n_mat
        power = _dot(n_mat.astype(BF16), n_mat.astype(BF16))
        inv = inv + _dot(inv.astype(BF16), power.astype(BF16))
        for _ in range(4):
            power = _dot(power.astype(BF16), power.astype(BF16))
            inv = inv + _dot(inv.astype(BF16), power.astype(BF16))

        rhs = jnp.concatenate([v_ref[r, :].astype(F32) * b_col, k_beta * exp_g], axis=1)
        uw = _dot(inv.astype(BF16), rhs.astype(BF16))
        dv = v_ref.shape[1]
        u = uw[:, :dv]
        w = uw[:, dv:]
        attn = jnp.where(tril, _dot_nt(q, k) * decay, 0.0)

        st = state[...]
        ws = _dot(jnp.concatenate([w.astype(BF16), q], axis=0), st.astype(BF16))
        v_new = (u - ws[:C]).astype(BF16)
        o = exp_g * ws[C:] + _dot(attn.astype(BF16), v_new)
        kd = (kf * jnp.exp(g_rem)).astype(BF16)
        state[...] = st * jnp.exp(g_last) + _dot_tn(kd, v_new)

        gate = z_ref[r, :].astype(F32)
        o_ref[r, :] = (_rms(o, gain_ref[...]) * _silu(gate)).astype(o_ref.dtype)


def _gdn_core(proj, bg, norm_gain, *, heads, dk, dv, rows=256):
    b, s, _ = proj.shape
    assert s % rows == 0 and rows % GDN_CHUNK == 0 and dv == 2 * dk and GDN_CHUNK == 64
    return pl.pallas_call(
        functools.partial(_gdn_kernel, rows=rows, chunk=GDN_CHUNK, heads=heads),
        grid=(b, heads, s // rows),
        in_specs=[
            pl.BlockSpec((None, rows, dk), lambda bi, h, c: (bi, c, h)),
            pl.BlockSpec((None, rows, dk), lambda bi, h, c: (bi, c, heads + h)),
            pl.BlockSpec((None, rows, dv), lambda bi, h, c: (bi, c, heads + h)),
            pl.BlockSpec((None, rows, dv), lambda bi, h, c: (bi, c, 2 * heads + h)),
            pl.BlockSpec((None, rows, LANES), lambda bi, h, c: (bi, c, 0)),
            pl.BlockSpec((1, dv), lambda bi, h, c: (0, 0)),
        ],
        out_specs=pl.BlockSpec((None, rows, dv), lambda bi, h, c: (bi, c, h)),
        out_shape=jax.ShapeDtypeStruct((b, s, heads * dv), BF16),
        scratch_shapes=[pltpu.VMEM((dk, dv), F32)],
        compiler_params=_params("parallel", "parallel", "arbitrary"),
        name="gdn_core",
    )(proj, proj, proj, proj, bg, norm_gain.reshape(1, dv))


def _sb_layer(x, gain, w_qkv, q_gain, k_gain, w_o):
    b, s, d = x.shape
    heads = w_qkv.shape[1] // (3 * SB_HEAD_DIM)
    x2 = x.reshape(b * s, d)
    qkv = _sb_proj(x2, gain, w_qkv.astype(BF16), q_gain, k_gain)
    o = _sb_attn(qkv.reshape(b, s, -1), heads=heads)
    return _oproj(o.reshape(b * s, -1), w_o.astype(BF16), x2).reshape(b, s, d)


def _ret_layer(x, gain, w_in, gn_gain, w_o):
    b, s, d = x.shape
    heads = RET_HEADS
    dv = gn_gain.shape[0]
    dk = dv // 2
    half = dk // 2
    inv = ROPE_BASE ** (-jnp.arange(half, dtype=F32) / half)
    ang = jnp.arange(s, dtype=F32)[:, None] * inv[None, :]
    log_gamma = jnp.log1p(-jnp.exp2(-5.0 - jnp.arange(heads, dtype=F32)))
    lg_rows = jnp.broadcast_to(log_gamma[:, None, None], (heads, 1, LANES))
    x2 = x.reshape(b * s, d)
    proj = _ret_proj(x2, gain, w_in.astype(BF16), jnp.cos(ang), jnp.sin(ang), seq=s, dk=dk)
    o = _ret_core(proj.reshape(b, s, -1), lg_rows, gn_gain, heads=heads, dk=dk, dv=dv)
    return _oproj(o.reshape(b * s, -1), w_o.astype(BF16), x2).reshape(b, s, d)


def _gdn_layer(x, gain, w_in, conv_w, a_log, dt_bias, norm_gain, w_o):
    b, s, d = x.shape
    heads = a_log.shape[0]
    dk = GDN_K_DIM
    dv = norm_gain.shape[0]
    n_main = 2 * heads * dk + 2 * heads * dv
    x2 = x.reshape(b * s, d)
    w_ab = jnp.zeros((d, LANES), BF16).at[:, :2 * heads].set(w_in[:, n_main:].astype(BF16))
    pad = lambda p: jnp.zeros((1, LANES), F32).at[0, :heads].set(p)
    proj = _gdn_proj(x2, gain, w_in[:, :n_main].astype(BF16), conv_w, seq=s, heads=heads, dk=dk)
    bg = _gdn_gate(x2, gain, w_ab, pad(a_log), pad(dt_bias), heads=heads)
    o = _gdn_core(proj.reshape(b, s, -1), bg.reshape(b, s, LANES), norm_gain, heads=heads, dk=dk, dv=dv)
    return _oproj(o.reshape(b * s, -1), w_o.astype(BF16), x2).reshape(b, s, d)


def _ffn_layer(x, gain, w_in, w_out):
    b, s, d = x.shape
    return _ffn(x.reshape(b * s, d), gain, w_in.astype(BF16), w_out.astype(BF16)).reshape(b, s, d)


def kernel(x, norm_mix, norm_ffn, ffn_w_in, ffn_w_out, pool_w, pool_scale, sb_w_qkv, sb_q_gain, sb_k_gain, sb_w_o, ret_w_in, ret_gn_gain, ret_w_o, gdn_w_in, gdn_conv_w, gdn_a_log, gdn_dt_bias, gdn_norm_gain, gdn_w_o):
    depth = norm_mix.shape[0]
    for layer in range(depth):
        mixer, occ = layer % 4, layer // 4
        if mixer == 0:
            x = _pool_layer(x, norm_mix[layer], pool_w[occ].astype(BF16), pool_scale[occ])
        elif mixer == 1:
            x = _sb_layer(x, norm_mix[layer], sb_w_qkv[occ], sb_q_gain[occ], sb_k_gain[occ], sb_w_o[occ])
        elif mixer == 2:
            x = _ret_layer(x, norm_mix[layer], ret_w_in[occ], ret_gn_gain[occ], ret_w_o[occ])
        else:
            x = _gdn_layer(x, norm_mix[layer], gdn_w_in[occ], gdn_conv_w[occ], gdn_a_log[occ],
                           gdn_dt_bias[occ], gdn_norm_gain[occ], gdn_w_o[occ])
        x = _ffn_layer(x, norm_ffn[layer], ffn_w_in[layer], ffn_w_out[layer])
    return x
```

```python
import functools

import jax
import jax.numpy as jnp
from jax import lax
from jax.experimental import pallas as pl
from jax.experimental.pallas import tpu as pltpu

F32 = jnp.float32
BF16 = jnp.bfloat16

EPS = 1e-6
ROPE_BASE = 10000.0
LANES = 128
SUBLANES = 8
VMEM_LIMIT = 56 * 1024 * 1024

POOL_WINDOWS = (2, 4, 8, 16)
POOL_HALO = 32
SB_HEAD_DIM = 128
RET_HEADS = 4
RET_CHUNK = 128
GDN_K_DIM = 128
GDN_CONV = 4
GDN_CHUNK = 64


def _params(*sem):
    return pltpu.CompilerParams(dimension_semantics=sem, vmem_limit_bytes=VMEM_LIMIT)


def _rms(xf, gain_row):
    ms = jnp.mean(xf * xf, axis=-1, keepdims=True)
    return xf * lax.rsqrt(ms + EPS) * gain_row


def _silu(x):
    return x * jax.nn.sigmoid(x)


def _dot(a, b):
    return jnp.dot(a, b, preferred_element_type=F32)


def _dot_nt(a, b):
    return lax.dot_general(a, b, (((1,), (1,)), ((), ())), preferred_element_type=F32)


def _dot_tn(a, b):
    return lax.dot_general(a, b, (((0,), (0,)), ((), ())), preferred_element_type=F32)


def _split_dot_rhs01(a_f32, b01):
    hi = a_f32.astype(BF16)
    lo = (a_f32 - hi.astype(F32)).astype(BF16)
    return _dot(hi, b01) + _dot(lo, b01)


def _split_dot_lhs01(a01, b_f32):
    hi = b_f32.astype(BF16)
    lo = (b_f32 - hi.astype(F32)).astype(BF16)
    return _dot(a01, hi) + _dot(a01, lo)


def _ffn_kernel(x_ref, g_ref, wg_ref, wu_ref, wo_ref, o_ref, h_scr, acc_scr):
    f = pl.program_id(1)

    @pl.when(f == 0)
    def _():
        h_scr[...] = _rms(x_ref[...], g_ref[...]).astype(BF16)
        acc_scr[...] = jnp.zeros_like(acc_scr)

    h = h_scr[...]
    gate = _dot(h, wg_ref[...])
    up = _dot(h, wu_ref[...])
    act = (_silu(gate) * up).astype(BF16)
    acc_scr[...] += _dot(act, wo_ref[...])

    @pl.when(f == pl.num_programs(1) - 1)
    def _():
        o_ref[...] = x_ref[...] + acc_scr[...]


def _ffn(x2, gain, w_in, w_out, *, tm=512, tf=1408):
    t, d = x2.shape
    f_hidden = w_out.shape[0]
    nf = f_hidden // tf
    assert t % tm == 0 and nf * tf == f_hidden
    return pl.pallas_call(
        _ffn_kernel,
        grid=(t // tm, nf),
        in_specs=[
            pl.BlockSpec((tm, d), lambda i, f: (i, 0)),
            pl.BlockSpec((1, d), lambda i, f: (0, 0)),
            pl.BlockSpec((d, tf), lambda i, f: (0, f)),
            pl.BlockSpec((d, tf), lambda i, f: (0, f + nf)),
            pl.BlockSpec((tf, d), lambda i, f: (f, 0)),
        ],
        out_specs=pl.BlockSpec((tm, d), lambda i, f: (i, 0)),
        out_shape=jax.ShapeDtypeStruct((t, d), F32),
        scratch_shapes=[pltpu.VMEM((tm, d), BF16), pltpu.VMEM((tm, d), F32)],
        compiler_params=_params("parallel", "arbitrary"),
        name="ffn",
    )(x2, gain.reshape(1, d), w_in, w_in, w_out)


def _oproj_kernel(a_ref, w_ref, x_ref, o_ref):
    o_ref[...] = x_ref[...] + _dot(a_ref[...], w_ref[...])


def _oproj(a2, w, x2, *, tm=512):
    t, k = a2.shape
    d = w.shape[1]
    assert t % tm == 0
    return pl.pallas_call(
        _oproj_kernel,
        grid=(t // tm,),
        in_specs=[
            pl.BlockSpec((tm, k), lambda i: (i, 0)),
            pl.BlockSpec((k, d), lambda i: (0, 0)),
            pl.BlockSpec((tm, d), lambda i: (i, 0)),
        ],
        out_specs=pl.BlockSpec((tm, d), lambda i: (i, 0)),
        out_shape=jax.ShapeDtypeStruct((t, d), F32),
        compiler_params=_params("parallel"),
        name="oproj",
    )(a2, w, x2)


def _pool_kernel(x_ref, g_ref, w_ref, sc_ref, o_ref, ext, buf_a, buf_b, *, ts, group):
    s = pl.program_id(1)
    n = ts + POOL_HALO
    d = x_ref.shape[-1]

    @pl.when(s == 0)
    def _():
        ext[0:POOL_HALO, :] = jnp.zeros((POOL_HALO, d), F32)

    x = x_ref[...]
    ext[POOL_HALO:n, :] = _rms(x, g_ref[...])

    src, dst = ext, buf_a
    for level in range(1, len(POOL_WINDOWS) + 1):
        shift = 1 << (level - 1)
        lo = SUBLANES * level
        c0 = (level - 1) * group
        dst[lo:n, c0:d] = src[lo:n, c0:d] + src[lo - shift:n - shift, c0:d]
        src, dst = dst, (buf_b if dst is buf_a else buf_a)

    pos1 = (s * ts + lax.broadcasted_iota(jnp.int32, (ts, 1), 0) + 1).astype(F32)
    for g, win in enumerate(POOL_WINDOWS):
        level_buf = buf_a if g % 2 == 0 else buf_b
        cols = slice(g * group, (g + 1) * group)
        inv_cnt = 1.0 / jnp.minimum(pos1, float(win))
        pooled = level_buf[POOL_HALO:n, cols] * inv_cnt - ext[POOL_HALO:n, cols]
        mixed = _dot(pooled.astype(BF16), w_ref[g])
        o_ref[:, cols] = x[:, cols] + mixed * sc_ref[:, cols]

    ext[0:POOL_HALO, :] = ext[ts:n, :]


def _pool_layer(x, gain, w_group, scale, *, ts=512):
    b, s, d = x.shape
    ngroups, group, _ = w_group.shape
    assert s % ts == 0 and ngroups == len(POOL_WINDOWS) and ngroups * group == d
    n = ts + POOL_HALO
    return pl.pallas_call(
        functools.partial(_pool_kernel, ts=ts, group=group),
        grid=(b, s // ts),
        in_specs=[
            pl.BlockSpec((None, ts, d), lambda i, j: (i, j, 0)),
            pl.BlockSpec((1, d), lambda i, j: (0, 0)),
            pl.BlockSpec((ngroups, group, group), lambda i, j: (0, 0, 0)),
            pl.BlockSpec((1, d), lambda i, j: (0, 0)),
        ],
        out_specs=pl.BlockSpec((None, ts, d), lambda i, j: (i, j, 0)),
        out_shape=jax.ShapeDtypeStruct((b, s, d), F32),
        scratch_shapes=[pltpu.VMEM((n, d), F32)] * 3,
        compiler_params=_params("arbitrary", "arbitrary"),
        name="pool_layer",
    )(x, gain.reshape(1, d), w_group, scale.reshape(1, d))


def _sb_proj_kernel(x_ref, g_ref, w_ref, qg_ref, kg_ref, o_ref, h_scr):
    j = pl.program_id(1)

    @pl.when(j == 0)
    def _():
        h_scr[...] = _rms(x_ref[...], g_ref[...]).astype(BF16)

    y = _dot(h_scr[...], w_ref[...])

    @pl.when(j < 2)
    def _():
        gain = jnp.where(j == 0, qg_ref[...] * (SB_HEAD_DIM ** -0.5), kg_ref[...])
        for c in range(y.shape[1] // SB_HEAD_DIM):
            cols = slice(c * SB_HEAD_DIM, (c + 1) * SB_HEAD_DIM)
            o_ref[:, cols] = _rms(y[:, cols], gain).astype(o_ref.dtype)

    @pl.when(j == 2)
    def _():
        o_ref[...] = y.astype(o_ref.dtype)


def _sb_proj(x2, gain, w_qkv, q_gain, k_gain, *, tm=512):
    t, d = x2.shape
    n = w_qkv.shape[1]
    tn = n // 3
    assert t % tm == 0 and tn % SB_HEAD_DIM == 0
    return pl.pallas_call(
        _sb_proj_kernel,
        grid=(t // tm, 3),
        in_specs=[
            pl.BlockSpec((tm, d), lambda i, j: (i, 0)),
            pl.BlockSpec((1, d), lambda i, j: (0, 0)),
            pl.BlockSpec((d, tn), lambda i, j: (0, j)),
            pl.BlockSpec((1, SB_HEAD_DIM), lambda i, j: (0, 0)),
            pl.BlockSpec((1, SB_HEAD_DIM), lambda i, j: (0, 0)),
        ],
        out_specs=pl.BlockSpec((tm, tn), lambda i, j: (i, j)),
        out_shape=jax.ShapeDtypeStruct((t, n), BF16),
        scratch_shapes=[pltpu.VMEM((tm, d), BF16)],
        compiler_params=_params("parallel", "arbitrary"),
        name="sb_proj",
    )(x2, gain.reshape(1, d), w_qkv, q_gain.reshape(1, -1), k_gain.reshape(1, -1))


def _sb_attn_kernel(q_ref, k_ref, v_ref, o_ref, *, tq, unroll):
    i = pl.program_id(2)
    tk = tq
    q = q_ref[...]
    row = lax.broadcasted_iota(jnp.int32, (tq, tk), 0)
    col = lax.broadcasted_iota(jnp.int32, (tq, tk), 1)
    causal = col < row
    jj = lax.broadcasted_iota(jnp.int32, (2 * tk, tk), 0)
    ss = lax.broadcasted_iota(jnp.int32, (2 * tk, tk), 1)
    suffix = jnp.where((jj % tk) > ss, 1.0, 0.0).astype(BF16)

    def scores(kb, masked):
        start = pl.multiple_of(kb * tk, tk)
        z = _dot_nt(q, k_ref[pl.ds(start, tk), :])
        lse = jnp.log(1.0 + jnp.exp(-jnp.abs(z)))
        log_1m_beta = -jnp.maximum(z, 0.0) - lse
        if masked:
            log_1m_beta = jnp.where(causal, log_1m_beta, 0.0)
        hi = log_1m_beta.astype(BF16)
        lo = (log_1m_beta - hi.astype(F32)).astype(BF16)
        log_beta = jnp.minimum(z, 0.0) - lse
        return jnp.concatenate([hi, lo], axis=1), log_beta, log_1m_beta[:, :1]

    def blocks(kbs, masks, acc, run):
        stage1 = [scores(kb, masked) for kb, masked in zip(kbs, masks)]
        afters = [_dot(split, suffix) for split, _, _ in stage1]
        for kb, masked, (_, log_beta, first_col), after in zip(kbs, masks, stage1, afters):
            wts = jnp.exp(log_beta + (after + run))
            if masked:
                wts = jnp.where(causal, wts, 0.0)
            start = pl.multiple_of(kb * tk, tk)
            acc = acc + _dot(wts.astype(BF16), v_ref[pl.ds(start, tk), :])
            run = run + (after[:, :1] + first_col)
        return acc, run

    rem = i % unroll

    def head(n_extra):
        def branch():
            acc = jnp.zeros((tq, q.shape[1]), F32)
            run = jnp.zeros((tq, 1), F32)
            kbs = [i] + [i - 1 - u for u in range(n_extra)]
            return blocks(kbs, [True] + [False] * n_extra, acc, run)
        return branch

    def group(t, carry):
        first = i - 1 - rem - unroll * t
        return blocks([first - u for u in range(unroll)], [False] * unroll, carry[0], carry[1])

    carry = lax.switch(rem, [head(n) for n in range(unroll)])
    acc, run = lax.fori_loop(0, i // unroll, group, carry)
    o_ref[...] = acc.astype(o_ref.dtype)


def _sb_attn(qkv, *, heads, tq=256, unroll=4):
    b, s, _ = qkv.shape
    dh = SB_HEAD_DIM
    assert s % tq == 0
    return pl.pallas_call(
        functools.partial(_sb_attn_kernel, tq=tq, unroll=unroll),
        grid=(b, heads, s // tq),
        in_specs=[
            pl.BlockSpec((None, tq, dh), lambda bi, h, i: (bi, i, h)),
            pl.BlockSpec((None, s, dh), lambda bi, h, i: (bi, 0, heads + h)),
            pl.BlockSpec((None, s, dh), lambda bi, h, i: (bi, 0, 2 * heads + h)),
        ],
        out_specs=pl.BlockSpec((None, tq, dh), lambda bi, h, i: (bi, i, h)),
        out_shape=jax.ShapeDtypeStruct((b, s, heads * dh), BF16),
        compiler_params=_params("parallel", "parallel", "arbitrary"),
        name="sb_attn",
    )(qkv, qkv, qkv)


def _ret_proj_kernel(x_ref, g_ref, w_ref, cos_ref, sin_ref, o_ref, h_scr, *, dk):
    j = pl.program_id(1)

    @pl.when(j == 0)
    def _():
        h_scr[...] = _rms(x_ref[...], g_ref[...]).astype(BF16)

    y = _dot(h_scr[...], w_ref[...])
    half = dk // 2

    @pl.when(j < 2)
    def _():
        scale = jnp.where(j == 0, 1.0, dk ** -0.5)
        cos = cos_ref[...] * scale
        sin = sin_ref[...] * scale
        for c in range(y.shape[1] // dk):
            x1 = y[:, c * dk:c * dk + half]
            x2 = y[:, c * dk + half:(c + 1) * dk]
            o_ref[:, c * dk:c * dk + half] = (x1 * cos - x2 * sin).astype(o_ref.dtype)
            o_ref[:, c * dk + half:(c + 1) * dk] = (x1 * sin + x2 * cos).astype(o_ref.dtype)

    @pl.when(j >= 2)
    def _():
        o_ref[...] = y.astype(o_ref.dtype)


def _ret_proj(x2, gain, w_in, cos, sin, *, seq, dk, tm=512, tn=1024):
    t, d = x2.shape
    n = w_in.shape[1]
    assert t % tm == 0 and seq % tm == 0 and n % tn == 0 and tn % dk == 0
    assert (RET_HEADS * dk) % tn == 0
    per_seq = seq // tm
    return pl.pallas_call(
        functools.partial(_ret_proj_kernel, dk=dk),
        grid=(t // tm, n // tn),
        in_specs=[
            pl.BlockSpec((tm, d), lambda i, j: (i, 0)),
            pl.BlockSpec((1, d), lambda i, j: (0, 0)),
            pl.BlockSpec((d, tn), lambda i, j: (0, j)),
            pl.BlockSpec((tm, dk // 2), lambda i, j: (i % per_seq, 0)),
            pl.BlockSpec((tm, dk // 2), lambda i, j: (i % per_seq, 0)),
        ],
        out_specs=pl.BlockSpec((tm, tn), lambda i, j: (i, j)),
        out_shape=jax.ShapeDtypeStruct((t, n), BF16),
        scratch_shapes=[pltpu.VMEM((tm, d), BF16)],
        compiler_params=_params("parallel", "arbitrary"),
        name="ret_proj",
    )(x2, gain.reshape(1, d), w_in, cos, sin)


def _ret_kernel(lg_ref, q_ref, k_ref, v_ref, g_ref, gn_ref, o_ref, state, *, rows, chunk):
    c = pl.program_id(2)

    @pl.when(c == 0)
    def _():
        state[...] = jnp.zeros_like(state)

    lg = lg_ref[...][:, :1]
    ri = lax.broadcasted_iota(jnp.int32, (chunk, chunk), 0)
    ci = lax.broadcasted_iota(jnp.int32, (chunk, chunk), 1)
    diff = (ri - ci).astype(F32)
    intra = jnp.where(diff >= 0, jnp.exp(jnp.maximum(diff, 0.0) * lg), 0.0)
    idx = lax.broadcasted_iota(jnp.int32, (chunk, 1), 0).astype(F32)
    q_dec = jnp.exp((idx + 1.0) * lg)
    k_dec = jnp.exp((chunk - 1.0 - idx) * lg)
    chunk_dec = jnp.exp(chunk * lg)

    for n in range(rows // chunk):
        r = slice(n * chunk, (n + 1) * chunk)
        q = q_ref[r, :]
        k = k_ref[r, :]
        v = v_ref[r, :]
        st = state[...]
        scores = _dot_nt(q, k) * intra
        o = _dot(scores.astype(BF16), v) + q_dec * _dot(q, st.astype(BF16))
        kd = (k.astype(F32) * k_dec).astype(BF16)
        state[...] = st * chunk_dec + _dot_tn(kd, v)
        gate = g_ref[r, :].astype(F32)
        o_ref[r, :] = (_rms(o, gn_ref[...]) * _silu(gate)).astype(o_ref.dtype)


def _ret_core(proj, lg_rows, gn_gain, *, heads, dk, dv, rows=512):
    b, s, _ = proj.shape
    assert s % rows == 0 and rows % RET_CHUNK == 0 and dv == 2 * dk
    return pl.pallas_call(
        functools.partial(_ret_kernel, rows=rows, chunk=RET_CHUNK),
        grid=(b, heads, s // rows),
        in_specs=[
            pl.BlockSpec((None, 1, LANES), lambda bi, h, c: (h, 0, 0)),
            pl.BlockSpec((None, rows, dk), lambda bi, h, c: (bi, c, h)),
            pl.BlockSpec((None, rows, dk), lambda bi, h, c: (bi, c, heads + h)),
            pl.BlockSpec((None, rows, dv), lambda bi, h, c: (bi, c, heads + h)),
            pl.BlockSpec((None, rows, dv), lambda bi, h, c: (bi, c, 2 * heads + h)),
            pl.BlockSpec((1, dv), lambda bi, h, c: (0, 0)),
        ],
        out_specs=pl.BlockSpec((None, rows, dv), lambda bi, h, c: (bi, c, h)),
        out_shape=jax.ShapeDtypeStruct((b, s, heads * dv), BF16),
        scratch_shapes=[pltpu.VMEM((dk, dv), F32)],
        compiler_params=_params("parallel", "parallel", "arbitrary"),
        name="ret_core",
    )(lg_rows, proj, proj, proj, proj, gn_gain.reshape(1, dv))


def _gdn_proj_kernel(x_ref, g_ref, w_ref, cw_ref, o_ref, h_scr, ybuf, halo, *, per_seq, n_conv, dk):
    i = pl.program_id(0)
    j = pl.program_id(1)
    tm = x_ref.shape[0]

    @pl.when(j == 0)
    def _():
        h_scr[...] = _rms(x_ref[...], g_ref[...]).astype(BF16)

    y = _dot(h_scr[...], w_ref[...])

    @pl.when(j < n_conv)
    def _():
        jc = jnp.minimum(j, n_conv - 1)

        @pl.when(i % per_seq == 0)
        def _():
            ybuf[0:SUBLANES, :] = jnp.zeros((SUBLANES, y.shape[1]), F32)

        @pl.when(i % per_seq != 0)
        def _():
            ybuf[0:SUBLANES, :] = halo[jc]

        ybuf[SUBLANES:SUBLANES + tm, :] = y
        halo[jc] = y[tm - SUBLANES:tm, :]
        conv = jnp.zeros_like(y)
        for tap in range(GDN_CONV):
            off = SUBLANES - (GDN_CONV - 1) + tap
            conv = conv + cw_ref[tap:tap + 1, :] * ybuf[off:off + tm, :]
        act = _silu(conv)

        @pl.when(j < 2)
        def _():
            scale = jnp.where(j == 0, dk ** -0.5, 1.0)
            for c in range(act.shape[1] // dk):
                cols = slice(c * dk, (c + 1) * dk)
                a = act[:, cols]
                nrm = lax.rsqrt(jnp.sum(a * a, axis=-1, keepdims=True) + EPS) * scale
                o_ref[:, cols] = (a * nrm).astype(o_ref.dtype)

        @pl.when(j >= 2)
        def _():
            o_ref[...] = act.astype(o_ref.dtype)

    @pl.when(j >= n_conv)
    def _():
        o_ref[...] = y.astype(o_ref.dtype)


def _gdn_proj(x2, gain, w_main, conv_w, *, seq, heads, dk, tm=512, tn=1024):
    t, d = x2.shape
    n = w_main.shape[1]
    n_conv = conv_w.shape[1] // tn
    assert t % tm == 0 and seq % tm == 0 and n % tn == 0 and conv_w.shape[1] % tn == 0
    assert heads * dk == tn
    per_seq = seq // tm
    return pl.pallas_call(
        functools.partial(_gdn_proj_kernel, per_seq=per_seq, n_conv=n_conv, dk=dk),
        grid=(t // tm, n // tn),
        in_specs=[
            pl.BlockSpec((tm, d), lambda i, j: (i, 0)),
            pl.BlockSpec((1, d), lambda i, j: (0, 0)),
            pl.BlockSpec((d, tn), lambda i, j: (0, j)),
            pl.BlockSpec((GDN_CONV, tn), lambda i, j: (0, jnp.minimum(j, n_conv - 1))),
        ],
        out_specs=pl.BlockSpec((tm, tn), lambda i, j: (i, j)),
        out_shape=jax.ShapeDtypeStruct((t, n), BF16),
        scratch_shapes=[
            pltpu.VMEM((tm, d), BF16),
            pltpu.VMEM((tm + SUBLANES, tn), F32),
            pltpu.VMEM((n_conv, SUBLANES, tn), F32),
        ],
        compiler_params=_params("arbitrary", "arbitrary"),
        name="gdn_proj",
    )(x2, gain.reshape(1, d), w_main, conv_w)


def _gdn_gate_kernel(x_ref, g_ref, w_ref, alog_ref, dtb_ref, o_ref, *, heads):
    h = _rms(x_ref[...], g_ref[...]).astype(BF16)
    y = _dot(h, w_ref[...])
    a = y + dtb_ref[...]
    softplus = jnp.maximum(a, 0.0) + jnp.log(1.0 + jnp.exp(-jnp.abs(a)))
    log_decay = -jnp.exp(alog_ref[...]) * softplus
    beta = jax.nn.sigmoid(y)
    lane = lax.broadcasted_iota(jnp.int32, y.shape, 1)
    o_ref[...] = jnp.where(lane < heads, log_decay, beta)


def _gdn_gate(x2, gain, w_ab, a_log_row, dt_bias_row, *, heads, tm=1024):
    t, d = x2.shape
    assert t % tm == 0
    return pl.pallas_call(
        functools.partial(_gdn_gate_kernel, heads=heads),
        grid=(t // tm,),
        in_specs=[
            pl.BlockSpec((tm, d), lambda i: (i, 0)),
            pl.BlockSpec((1, d), lambda i: (0, 0)),
            pl.BlockSpec((d, LANES), lambda i: (0, 0)),
            pl.BlockSpec((1, LANES), lambda i: (0, 0)),
            pl.BlockSpec((1, LANES), lambda i: (0, 0)),
        ],
        out_specs=pl.BlockSpec((tm, LANES), lambda i: (i, 0)),
        out_shape=jax.ShapeDtypeStruct((t, LANES), F32),
        compiler_params=_params("parallel"),
        name="gdn_gate",
    )(x2, gain.reshape(1, d), w_ab, a_log_row, dt_bias_row)


def _gdn_kernel(q_ref, k_ref, v_ref, z_ref, bg_ref, gain_ref, o_ref, state, *, rows, chunk, heads, hp):
    hb = pl.program_id(1)
    c = pl.program_id(2)
    C = chunk
    nc = rows // C
    dk = q_ref.shape[1] // hp
    dv = v_ref.shape[1] // hp

    @pl.when(c == 0)
    def _():
        state[...] = jnp.zeros_like(state)

    lane = lax.broadcasted_iota(jnp.int32, (C, LANES), 1)
    ri = lax.broadcasted_iota(jnp.int32, (C, C), 0)
    ci = lax.broadcasted_iota(jnp.int32, (C, C), 1)
    tril = ci <= ri
    strict = ci < ri
    eye = jnp.where(ci == ri, 1.0, 0.0)
    ai = lax.broadcasted_iota(jnp.int32, (2 * C, C), 0)
    am = lax.broadcasted_iota(jnp.int32, (2 * C, C), 1)
    sum_sel = jnp.where(((ai < C) & (am <= ai)) | ((ai >= C) & (am > ai - C)), 1.0, 0.0).astype(BF16)
    bm = lax.broadcasted_iota(jnp.int32, (C, 2 * C), 0)
    bj = lax.broadcasted_iota(jnp.int32, (C, 2 * C), 1)
    col_sel = jnp.where((bm > bj) | (bj >= C), 1.0, 0.0)

    items = [(n, h) for n in range(nc) for h in range(hp)]
    r_of = lambda n: slice(n * C, (n + 1) * C)
    c_of = lambda h, w: slice(h * w, (h + 1) * w)
    bgs = [bg_ref[r_of(n), :] for n in range(nc)]
    pick = lambda n, ln: jnp.sum(jnp.where(lane == ln, bgs[n], 0.0), axis=-1, keepdims=True)
    g_col = [pick(n, hb * hp + h) for n, h in items]
    b_col = [pick(n, hb * hp + h + heads) for n, h in items]

    gmat = [g * col_sel for g in g_col]
    g_hi = [m.astype(BF16) for m in gmat]
    g_lo = [(m - hi.astype(F32)).astype(BF16) for m, hi in zip(gmat, g_hi)]
    sums_hi = [_dot(sum_sel, hi) for hi in g_hi]
    sums_lo = [_dot(sum_sel, lo) for lo in g_lo]
    sums = [a + b for a, b in zip(sums_hi, sums_lo)]
    decay = [jnp.where(tril, jnp.exp(s[:C, :C]), 0.0) for s in sums]
    g_cum = [s[:C, C:C + 1] for s in sums]
    exp_g = [jnp.exp(g) for g in g_cum]
    exp_rem = [jnp.exp(s[C:, C:C + 1]) for s in sums]
    exp_last = [e[C - 1:C, :] for e in exp_g]

    q = [q_ref[r_of(n), c_of(h, dk)] for n, h in items]
    k = [k_ref[r_of(n), c_of(h, dk)] for n, h in items]
    kf = [x.astype(F32) for x in k]
    k_beta = [x * b for x, b in zip(kf, b_col)]
    kk = [_dot_nt(kb.astype(BF16), x) for kb, x in zip(k_beta, k)]
    n_mat = [jnp.where(strict, a * d, 0.0) for a, d in zip(kk, decay)]
    inv = [eye - m for m in n_mat]
    n_bf = [m.astype(BF16) for m in n_mat]
    power = [_dot(m, m) for m in n_bf]
    inv = [x + _dot(x.astype(BF16), p.astype(BF16)) for x, p in zip(inv, power)]
    for _ in range(4):
        power = [_dot(p.astype(BF16), p.astype(BF16)) for p in power]
        inv = [x + _dot(x.astype(BF16), p.astype(BF16)) for x, p in zip(inv, power)]

    rhs = [jnp.concatenate([v_ref[r_of(n), c_of(h, dv)].astype(F32) * b, kb * eg], axis=1).astype(BF16)
           for (n, h), b, kb, eg in zip(items, b_col, k_beta, exp_g)]
    uw = [_dot(x.astype(BF16), r) for x, r in zip(inv, rhs)]
    qk = [_dot_nt(a, b) for a, b in zip(q, k)]
    attn = [jnp.where(tril, a * d, 0.0).astype(BF16) for a, d in zip(qk, decay)]
    kd = [(x * e).astype(BF16) for x, e in zip(kf, exp_rem)]

    st = [state[h] for h in range(hp)]
    for n in range(nc):
        idx = [n * hp + h for h in range(hp)]
        wq = [jnp.concatenate([uw[i][:, dv:].astype(BF16), q[i]], axis=0) for i in idx]
        ws = [_dot(a, s.astype(BF16)) for a, s in zip(wq, st)]
        v_new = [(uw[i][:, :dv] - w[:C]).astype(BF16) for i, w in zip(idx, ws)]
        o = [exp_g[i] * w[C:] + _dot(attn[i], vn) for i, w, vn in zip(idx, ws, v_new)]
        st = [s * exp_last[i] + _dot_tn(kd[i], vn) for i, s, vn in zip(idx, st, v_new)]
        for h in range(hp):
            gate = z_ref[r_of(n), c_of(h, dv)].astype(F32)
            o_ref[r_of(n), c_of(h, dv)] = (_rms(o[h], gain_ref[...]) * _silu(gate)).astype(o_ref.dtype)
    for h in range(hp):
        state[h] = st[h]


def _gdn_core(proj, bg, norm_gain, *, heads, dk, dv, rows=256, hp=8):
    b, s, _ = proj.shape
    assert s % rows == 0 and rows % GDN_CHUNK == 0 and dv == 2 * dk and GDN_CHUNK == 64
    assert heads % hp == 0
    nhb = heads // hp
    return pl.pallas_call(
        functools.partial(_gdn_kernel, rows=rows, chunk=GDN_CHUNK, heads=heads, hp=hp),
        grid=(b, nhb, s // rows),
        in_specs=[
            pl.BlockSpec((None, rows, hp * dk), lambda bi, h, c: (bi, c, h)),
            pl.BlockSpec((None, rows, hp * dk), lambda bi, h, c: (bi, c, nhb + h)),
            pl.BlockSpec((None, rows, hp * dv), lambda bi, h, c: (bi, c, nhb + h)),
            pl.BlockSpec((None, rows, hp * dv), lambda bi, h, c: (bi, c, 2 * nhb + h)),
            pl.BlockSpec((None, rows, LANES), lambda bi, h, c: (bi, c, 0)),
            pl.BlockSpec((1, dv), lambda bi, h, c: (0, 0)),
        ],
        out_specs=pl.BlockSpec((None, rows, hp * dv), lambda bi, h, c: (bi, c, h)),
        out_shape=jax.ShapeDtypeStruct((b, s, heads * dv), BF16),
        scratch_shapes=[pltpu.VMEM((hp, dk, dv), F32)],
        compiler_params=_params("parallel", "parallel", "arbitrary"),
        name="gdn_core",
    )(proj, proj, proj, proj, bg, norm_gain.reshape(1, dv))


def _sb_layer(x, gain, w_qkv, q_gain, k_gain, w_o):
    b, s, d = x.shape
    heads = w_qkv.shape[1] // (3 * SB_HEAD_DIM)
    x2 = x.reshape(b * s, d)
    qkv = _sb_proj(x2, gain, w_qkv.astype(BF16), q_gain, k_gain)
    o = _sb_attn(qkv.reshape(b, s, -1), heads=heads)
    return _oproj(o.reshape(b * s, -1), w_o.astype(BF16), x2).reshape(b, s, d)


def _ret_layer(x, gain, w_in, gn_gain, w_o):
    b, s, d = x.shape
    heads = RET_HEADS
    dv = gn_gain.shape[0]
    dk = dv // 2
    half = dk // 2
    inv = ROPE_BASE ** (-jnp.arange(half, dtype=F32) / half)
    ang = jnp.arange(s, dtype=F32)[:, None] * inv[None, :]
    log_gamma = jnp.log1p(-jnp.exp2(-5.0 - jnp.arange(heads, dtype=F32)))
    lg_rows = jnp.broadcast_to(log_gamma[:, None, None], (heads, 1, LANES))
    x2 = x.reshape(b * s, d)
    proj = _ret_proj(x2, gain, w_in.astype(BF16), jnp.cos(ang), jnp.sin(ang), seq=s, dk=dk)
    o = _ret_core(proj.reshape(b, s, -1), lg_rows, gn_gain, heads=heads, dk=dk, dv=dv)
    return _oproj(o.reshape(b * s, -1), w_o.astype(BF16), x2).reshape(b, s, d)


def _gdn_layer(x, gain, w_in, conv_w, a_log, dt_bias, norm_gain, w_o):
    b, s, d = x.shape
    heads = a_log.shape[0]
    dk = GDN_K_DIM
    dv = norm_gain.shape[0]
    n_main = 2 * heads * dk + 2 * heads * dv
    x2 = x.reshape(b * s, d)
    w_ab = jnp.zeros((d, LANES), BF16).at[:, :2 * heads].set(w_in[:, n_main:].astype(BF16))
    pad = lambda p: jnp.zeros((1, LANES), F32).at[0, :heads].set(p)
    proj = _gdn_proj(x2, gain, w_in[:, :n_main].astype(BF16), conv_w, seq=s, heads=heads, dk=dk)
    bg = _gdn_gate(x2, gain, w_ab, pad(a_log), pad(dt_bias), heads=heads)
    o = _gdn_core(proj.reshape(b, s, -1), bg.reshape(b, s, LANES), norm_gain, heads=heads, dk=dk, dv=dv)
    return _oproj(o.reshape(b * s, -1), w_o.astype(BF16), x2).reshape(b, s, d)


def _ffn_layer(x, gain, w_in, w_out):
    b, s, d = x.shape
    return _ffn(x.reshape(b * s, d), gain, w_in.astype(BF16), w_out.astype(BF16)).reshape(b, s, d)


def kernel(x, norm_mix, norm_ffn, ffn_w_in, ffn_w_out, pool_w, pool_scale, sb_w_qkv, sb_q_gain, sb_k_gain, sb_w_o, ret_w_in, ret_gn_gain, ret_w_o, gdn_w_in, gdn_conv_w, gdn_a_log, gdn_dt_bias, gdn_norm_gain, gdn_w_o):
    depth = norm_mix.shape[0]
    for layer in range(depth):
        mixer, occ = layer % 4, layer // 4
        if mixer == 0:
            x = _pool_layer(x, norm_mix[layer], pool_w[occ].astype(BF16), pool_scale[occ])
        elif mixer == 1:
            x = _sb_layer(x, norm_mix[layer], sb_w_qkv[occ], sb_q_gain[occ], sb_k_gain[occ], sb_w_o[occ])
        elif mixer == 2:
            x = _ret_layer(x, norm_mix[layer], ret_w_in[occ], ret_gn_gain[occ], ret_w_o[occ])
        else:
            x = _gdn_layer(x, norm_mix[layer], gdn_w_in[occ], gdn_conv_w[occ], gdn_a_log[occ],
                           gdn_dt_bias[occ], gdn_norm_gain[occ], gdn_w_o[occ])
        x = _ffn_layer(x, norm_ffn[layer], ffn_w_in[layer], ffn_w_out[layer])
    return x
```

```python
import functools

import jax
import jax.numpy as jnp
from jax import lax
from jax.experimental import pallas as pl
from jax.experimental.pallas import tpu as pltpu

F32 = jnp.float32
BF16 = jnp.bfloat16

EPS = 1e-6
LOG2E = 1.4426950408889634
SOFTPLUS_LINEAR = 64.0
ROPE_BASE = 10000.0
LANES = 128
SUBLANES = 8
VMEM_LIMIT = 56 * 1024 * 1024
PROJ_TILE = 256

POOL_WINDOWS = (2, 4, 8, 16)
POOL_HALO = 32
SB_HEAD_DIM = 128
RET_HEADS = 4
RET_CHUNK = 128
GDN_K_DIM = 128
GDN_CONV = 4
GDN_CHUNK = 64
CONV_ROWS = 64


def _params(*sem):
    return pltpu.CompilerParams(dimension_semantics=sem, vmem_limit_bytes=VMEM_LIMIT)


def _resident(shape):
    return pl.BlockSpec(shape, lambda *_: (0,) * len(shape), pipeline_mode=pl.Buffered(1))


def _rms(xf, gain_row):
    ms = jnp.mean(xf * xf, axis=-1, keepdims=True)
    return xf * lax.rsqrt(ms + EPS) * gain_row


def _silu(x):
    return x * jax.nn.sigmoid(x)


def _dot(a, b):
    return jnp.dot(a, b, preferred_element_type=F32)


def _dot_nt(a, b):
    return lax.dot_general(a, b, (((1,), (1,)), ((), ())), preferred_element_type=F32)


def _dot_tn(a, b):
    return lax.dot_general(a, b, (((0,), (0,)), ((), ())), preferred_element_type=F32)


def _split_dot_rhs01(a_f32, b01):
    hi = a_f32.astype(BF16)
    lo = (a_f32 - hi.astype(F32)).astype(BF16)
    return _dot(hi, b01) + _dot(lo, b01)


def _split_dot_lhs01(a01, b_f32):
    hi = b_f32.astype(BF16)
    lo = (b_f32 - hi.astype(F32)).astype(BF16)
    return _dot(a01, hi) + _dot(a01, lo)


def _ffn_kernel(x_ref, g_ref, wg_ref, wu_ref, wo_ref, o_ref, h_scr, acc_scr):
    f = pl.program_id(1)

    @pl.when(f == 0)
    def _():
        h_scr[...] = _rms(x_ref[...], g_ref[...]).astype(BF16)
        acc_scr[...] = jnp.zeros_like(acc_scr)

    h = h_scr[...]
    gate = _dot(h, wg_ref[...])
    up = _dot(h, wu_ref[...])
    act = (_silu(gate) * up).astype(BF16)
    acc_scr[...] += _dot(act, wo_ref[...])

    @pl.when(f == pl.num_programs(1) - 1)
    def _():
        o_ref[...] = x_ref[...] + acc_scr[...]


def _ffn(x2, gain, w_in, w_out, *, tm=512, tf=1408):
    t, d = x2.shape
    f_hidden = w_out.shape[0]
    nf = f_hidden // tf
    assert t % tm == 0 and nf * tf == f_hidden
    return pl.pallas_call(
        _ffn_kernel,
        grid=(t // tm, nf),
        in_specs=[
            pl.BlockSpec((tm, d), lambda i, f: (i, 0)),
            pl.BlockSpec((1, d), lambda i, f: (0, 0)),
            pl.BlockSpec((d, tf), lambda i, f: (0, f)),
            pl.BlockSpec((d, tf), lambda i, f: (0, f + nf)),
            pl.BlockSpec((tf, d), lambda i, f: (f, 0)),
        ],
        out_specs=pl.BlockSpec((tm, d), lambda i, f: (i, 0)),
        out_shape=jax.ShapeDtypeStruct((t, d), F32),
        scratch_shapes=[pltpu.VMEM((tm, d), BF16), pltpu.VMEM((tm, d), F32)],
        compiler_params=_params("parallel", "arbitrary"),
        name="ffn",
    )(x2, gain.reshape(1, d), w_in, w_in, w_out)


def _oproj_kernel(a_ref, w_ref, x_ref, o_ref):
    o_ref[...] = x_ref[...] + _dot(a_ref[...], w_ref[...])


def _oproj(a2, w, x2, *, tm=512):
    t, k = a2.shape
    d = w.shape[1]
    assert t % tm == 0
    return pl.pallas_call(
        _oproj_kernel,
        grid=(t // tm,),
        in_specs=[
            pl.BlockSpec((tm, k), lambda i: (i, 0)),
            pl.BlockSpec((k, d), lambda i: (0, 0)),
            pl.BlockSpec((tm, d), lambda i: (i, 0)),
        ],
        out_specs=pl.BlockSpec((tm, d), lambda i: (i, 0)),
        out_shape=jax.ShapeDtypeStruct((t, d), F32),
        compiler_params=_params("parallel"),
        name="oproj",
    )(a2, w, x2)


def _pool_kernel(x_ref, g_ref, w_ref, sc_ref, o_ref, ext, buf_a, buf_b, *, ts, group):
    s = pl.program_id(1)
    n = ts + POOL_HALO
    d = x_ref.shape[-1]

    @pl.when(s == 0)
    def _():
        ext[0:POOL_HALO, :] = jnp.zeros((POOL_HALO, d), F32)

    x = x_ref[...]
    ext[POOL_HALO:n, :] = _rms(x, g_ref[...])

    src, dst = ext, buf_a
    for level in range(1, len(POOL_WINDOWS) + 1):
        shift = 1 << (level - 1)
        lo = SUBLANES * level
        c0 = (level - 1) * group
        dst[lo:n, c0:d] = src[lo:n, c0:d] + src[lo - shift:n - shift, c0:d]
        src, dst = dst, (buf_b if dst is buf_a else buf_a)

    pos1 = (s * ts + lax.broadcasted_iota(jnp.int32, (ts, 1), 0) + 1).astype(F32)
    for g, win in enumerate(POOL_WINDOWS):
        level_buf = buf_a if g % 2 == 0 else buf_b
        cols = slice(g * group, (g + 1) * group)
        inv_cnt = 1.0 / jnp.minimum(pos1, float(win))
        pooled = level_buf[POOL_HALO:n, cols] * inv_cnt - ext[POOL_HALO:n, cols]
        mixed = _dot(pooled.astype(BF16), w_ref[g])
        o_ref[:, cols] = x[:, cols] + mixed * sc_ref[:, cols]

    ext[0:POOL_HALO, :] = ext[ts:n, :]


def _pool_layer(x, gain, w_group, scale, *, ts=512):
    b, s, d = x.shape
    ngroups, group, _ = w_group.shape
    assert s % ts == 0 and ngroups == len(POOL_WINDOWS) and ngroups * group == d
    n = ts + POOL_HALO
    return pl.pallas_call(
        functools.partial(_pool_kernel, ts=ts, group=group),
        grid=(b, s // ts),
        in_specs=[
            pl.BlockSpec((None, ts, d), lambda i, j: (i, j, 0)),
            pl.BlockSpec((1, d), lambda i, j: (0, 0)),
            pl.BlockSpec((ngroups, group, group), lambda i, j: (0, 0, 0)),
            pl.BlockSpec((1, d), lambda i, j: (0, 0)),
        ],
        out_specs=pl.BlockSpec((None, ts, d), lambda i, j: (i, j, 0)),
        out_shape=jax.ShapeDtypeStruct((b, s, d), F32),
        scratch_shapes=[pltpu.VMEM((n, d), F32)] * 3,
        compiler_params=_params("arbitrary", "arbitrary"),
        name="pool_layer",
    )(x, gain.reshape(1, d), w_group, scale.reshape(1, d))


def _sb_proj_kernel(x_ref, g_ref, w_ref, qg_ref, kg_ref, o_ref, h_scr, *, width):
    h_scr[...] = _rms(x_ref[...], g_ref[...]).astype(BF16)
    gains = (qg_ref[...] * (SB_HEAD_DIM ** -0.5 * LOG2E), kg_ref[...], None)
    for tile in range(w_ref.shape[1] // PROJ_TILE):
        cols = slice(tile * PROJ_TILE, (tile + 1) * PROJ_TILE)
        y = _dot(h_scr[...], w_ref[:, cols])
        gain = gains[tile * PROJ_TILE // width]
        if gain is None:
            o_ref[:, cols] = y.astype(o_ref.dtype)
        else:
            for c in range(PROJ_TILE // SB_HEAD_DIM):
                sub = slice(c * SB_HEAD_DIM, (c + 1) * SB_HEAD_DIM)
                o_ref[:, tile * PROJ_TILE + c * SB_HEAD_DIM:tile * PROJ_TILE + (c + 1) * SB_HEAD_DIM] = (
                    _rms(y[:, sub], gain).astype(o_ref.dtype))


def _sb_proj(x2, gain, w_qkv, q_gain, k_gain, *, tm=512):
    t, d = x2.shape
    n = w_qkv.shape[1]
    width = n // 3
    assert t % tm == 0 and width % PROJ_TILE == 0 and PROJ_TILE % SB_HEAD_DIM == 0
    return pl.pallas_call(
        functools.partial(_sb_proj_kernel, width=width),
        grid=(t // tm,),
        in_specs=[
            pl.BlockSpec((tm, d), lambda i: (i, 0)),
            _resident((1, d)),
            _resident((d, n)),
            _resident((1, SB_HEAD_DIM)),
            _resident((1, SB_HEAD_DIM)),
        ],
        out_specs=pl.BlockSpec((tm, n), lambda i: (i, 0)),
        out_shape=jax.ShapeDtypeStruct((t, n), BF16),
        scratch_shapes=[pltpu.VMEM((tm, d), BF16)],
        compiler_params=_params("parallel"),
        name="sb_proj",
    )(x2, gain.reshape(1, d), w_qkv, q_gain.reshape(1, -1), k_gain.reshape(1, -1))


def _sb_attn_kernel(q_ref, k_ref, v_ref, o_ref, *, tq, unroll):
    i = pl.program_id(2)
    tk = tq
    q = q_ref[...]
    row = lax.broadcasted_iota(jnp.int32, (tq, tk), 0)
    col = lax.broadcasted_iota(jnp.int32, (tq, tk), 1)
    causal = col < row
    jj = lax.broadcasted_iota(jnp.int32, (tk, tk), 0)
    ss = lax.broadcasted_iota(jnp.int32, (tk, tk), 1)
    suffix = jnp.where(jj > ss, 1.0, 0.0).astype(BF16)

    def scores(kb, masked):
        start = pl.multiple_of(kb * tk, tk)
        z = _dot_nt(q, k_ref[pl.ds(start, tk), :])
        p = jnp.where(z > SOFTPLUS_LINEAR, z, jnp.log(1.0 + jnp.exp2(z)) * LOG2E)
        log_beta = z - p
        if masked:
            p = jnp.where(causal, p, 0.0)
        return p.astype(BF16), log_beta, p[:, :1]

    def blocks(kbs, masks, acc, run):
        stage1 = [scores(kb, masked) for kb, masked in zip(kbs, masks)]
        afters = [_dot(split, suffix) for split, _, _ in stage1]
        for kb, masked, (_, log_beta, first_col), after in zip(kbs, masks, stage1, afters):
            wts = jnp.exp2(log_beta - (after + run))
            if masked:
                wts = jnp.where(causal, wts, 0.0)
            start = pl.multiple_of(kb * tk, tk)
            acc = acc + _dot(wts.astype(BF16), v_ref[pl.ds(start, tk), :])
            run = run + (after[:, :1] + first_col)
        return acc, run

    rem = i % unroll

    def head(n_extra):
        def branch():
            acc = jnp.zeros((tq, q.shape[1]), F32)
            run = jnp.zeros((tq, 1), F32)
            kbs = [i] + [i - 1 - u for u in range(n_extra)]
            return blocks(kbs, [True] + [False] * n_extra, acc, run)
        return branch

    def group(t, carry):
        first = i - 1 - rem - unroll * t
        return blocks([first - u for u in range(unroll)], [False] * unroll, carry[0], carry[1])

    carry = lax.switch(rem, [head(n) for n in range(unroll)])
    acc, run = lax.fori_loop(0, i // unroll, group, carry)
    o_ref[...] = acc.astype(o_ref.dtype)


def _sb_attn(qkv, *, heads, tq=256, unroll=8):
    b, s, _ = qkv.shape
    dh = SB_HEAD_DIM
    assert s % tq == 0
    return pl.pallas_call(
        functools.partial(_sb_attn_kernel, tq=tq, unroll=unroll),
        grid=(b, heads, s // tq),
        in_specs=[
            pl.BlockSpec((None, tq, dh), lambda bi, h, i: (bi, i, h)),
            pl.BlockSpec((None, s, dh), lambda bi, h, i: (bi, 0, heads + h)),
            pl.BlockSpec((None, s, dh), lambda bi, h, i: (bi, 0, 2 * heads + h)),
        ],
        out_specs=pl.BlockSpec((None, tq, dh), lambda bi, h, i: (bi, i, h)),
        out_shape=jax.ShapeDtypeStruct((b, s, heads * dh), BF16),
        compiler_params=_params("parallel", "parallel", "arbitrary"),
        name="sb_attn",
    )(qkv, qkv, qkv)


def _ret_proj_kernel(x_ref, g_ref, w_ref, cos_ref, sin_ref, o_ref, h_scr, *, dk, heads):
    h_scr[...] = _rms(x_ref[...], g_ref[...]).astype(BF16)
    half = dk // 2
    cos = cos_ref[...]
    sin = sin_ref[...]
    for tile in range(w_ref.shape[1] // dk):
        c0 = tile * dk
        y = _dot(h_scr[...], w_ref[:, c0:c0 + dk])
        if tile < 2 * heads:
            if tile >= heads:
                y = y * (dk ** -0.5)
            x1 = y[:, :half]
            x2 = y[:, half:]
            o_ref[:, c0:c0 + half] = (x1 * cos - x2 * sin).astype(o_ref.dtype)
            o_ref[:, c0 + half:c0 + dk] = (x1 * sin + x2 * cos).astype(o_ref.dtype)
        else:
            o_ref[:, c0:c0 + dk] = y.astype(o_ref.dtype)


def _ret_proj(x2, gain, w_in, cos, sin, *, seq, dk, heads, tm=512):
    t, d = x2.shape
    n = w_in.shape[1]
    assert t % tm == 0 and seq % tm == 0 and n % dk == 0 and dk == PROJ_TILE
    per_seq = seq // tm
    return pl.pallas_call(
        functools.partial(_ret_proj_kernel, dk=dk, heads=heads),
        grid=(t // tm,),
        in_specs=[
            pl.BlockSpec((tm, d), lambda i: (i, 0)),
            _resident((1, d)),
            _resident((d, n)),
            pl.BlockSpec((tm, dk // 2), lambda i: (i % per_seq, 0)),
            pl.BlockSpec((tm, dk // 2), lambda i: (i % per_seq, 0)),
        ],
        out_specs=pl.BlockSpec((tm, n), lambda i: (i, 0)),
        out_shape=jax.ShapeDtypeStruct((t, n), BF16),
        scratch_shapes=[pltpu.VMEM((tm, d), BF16)],
        compiler_params=_params("parallel"),
        name="ret_proj",
    )(x2, gain.reshape(1, d), w_in, cos, sin)


def _ret_kernel(lg_ref, q_ref, k_ref, v_ref, g_ref, gn_ref, o_ref, state, *, rows, chunk):
    c = pl.program_id(2)

    @pl.when(c == 0)
    def _():
        state[...] = jnp.zeros_like(state)

    lg = lg_ref[...][:, :1]
    ri = lax.broadcasted_iota(jnp.int32, (chunk, chunk), 0)
    ci = lax.broadcasted_iota(jnp.int32, (chunk, chunk), 1)
    diff = (ri - ci).astype(F32)
    intra = jnp.where(diff >= 0, jnp.exp(jnp.maximum(diff, 0.0) * lg), 0.0)
    idx = lax.broadcasted_iota(jnp.int32, (chunk, 1), 0).astype(F32)
    q_dec = jnp.exp((idx + 1.0) * lg)
    k_dec = jnp.exp((chunk - 1.0 - idx) * lg)
    chunk_dec = jnp.exp(chunk * lg)

    for n in range(rows // chunk):
        r = slice(n * chunk, (n + 1) * chunk)
        q = q_ref[r, :]
        k = k_ref[r, :]
        v = v_ref[r, :]
        st = state[...]
        scores = _dot_nt(q, k) * intra
        o = _dot(scores.astype(BF16), v) + q_dec * _dot(q, st.astype(BF16))
        kd = (k.astype(F32) * k_dec).astype(BF16)
        state[...] = st * chunk_dec + _dot_tn(kd, v)
        gate = g_ref[r, :].astype(F32)
        o_ref[r, :] = (_rms(o, gn_ref[...]) * _silu(gate)).astype(o_ref.dtype)


def _ret_core(proj, lg_rows, gn_gain, *, heads, dk, dv, rows=512):
    b, s, _ = proj.shape
    assert s % rows == 0 and rows % RET_CHUNK == 0 and dv == 2 * dk
    return pl.pallas_call(
        functools.partial(_ret_kernel, rows=rows, chunk=RET_CHUNK),
        grid=(b, heads, s // rows),
        in_specs=[
            pl.BlockSpec((None, 1, LANES), lambda bi, h, c: (h, 0, 0)),
            pl.BlockSpec((None, rows, dk), lambda bi, h, c: (bi, c, h)),
            pl.BlockSpec((None, rows, dk), lambda bi, h, c: (bi, c, heads + h)),
            pl.BlockSpec((None, rows, dv), lambda bi, h, c: (bi, c, heads + h)),
            pl.BlockSpec((None, rows, dv), lambda bi, h, c: (bi, c, 2 * heads + h)),
            pl.BlockSpec((1, dv), lambda bi, h, c: (0, 0)),
        ],
        out_specs=pl.BlockSpec((None, rows, dv), lambda bi, h, c: (bi, c, h)),
        out_shape=jax.ShapeDtypeStruct((b, s, heads * dv), BF16),
        scratch_shapes=[pltpu.VMEM((dk, dv), F32)],
        compiler_params=_params("parallel", "parallel", "arbitrary"),
        name="ret_core",
    )(lg_rows, proj, proj, proj, proj, gn_gain.reshape(1, dv))


def _gdn_proj_kernel(x_ref, g_ref, w_ref, wab_ref, cw_ref, alog_ref, dtb_ref, o_ref, bg_ref,
                     h_scr, halo, *, per_seq, heads, dk):
    i = pl.program_id(0)
    tm = x_ref.shape[0]
    n_conv = cw_ref.shape[1]

    @pl.when(i == 0)
    def _():
        halo[...] = jnp.zeros_like(halo)

    h_scr[...] = _rms(x_ref[...], g_ref[...]).astype(BF16)
    first = (i % per_seq) == 0
    row8 = lax.broadcasted_iota(jnp.int32, (SUBLANES, PROJ_TILE), 0)
    for tile in range(w_ref.shape[1] // PROJ_TILE):
        c0 = tile * PROJ_TILE
        cols = slice(c0, c0 + PROJ_TILE)
        y = _dot(h_scr[...], w_ref[:, cols])
        if c0 >= n_conv:
            o_ref[:, cols] = y.astype(o_ref.dtype)
            continue
        prev = jnp.where(first, 0.0, halo[:, cols])
        halo[:, cols] = y[tm - SUBLANES:tm, :]
        for r0 in range(0, tm, CONV_ROWS):
            rows = slice(r0, r0 + CONV_ROWS)
            yc = y[rows]
            conv = cw_ref[GDN_CONV - 1:GDN_CONV, cols] * yc
            for back in range(1, GDN_CONV):
                shifted = pltpu.roll(yc, back, axis=0)
                top = jnp.where(row8 < back, pltpu.roll(prev, back, axis=0), shifted[:SUBLANES])
                shifted = jnp.concatenate([top, shifted[SUBLANES:]], axis=0)
                conv = conv + cw_ref[GDN_CONV - 1 - back:GDN_CONV - back, cols] * shifted
            prev = yc[CONV_ROWS - SUBLANES:]
            act = _silu(conv)
            if c0 >= 2 * heads * dk:
                o_ref[rows, cols] = act.astype(o_ref.dtype)
                continue
            scale = dk ** -0.5 if c0 < heads * dk else 1.0
            for c in range(PROJ_TILE // dk):
                a = act[:, c * dk:(c + 1) * dk]
                nrm = lax.rsqrt(jnp.sum(a * a, axis=-1, keepdims=True) + EPS) * scale
                o_ref[rows, c0 + c * dk:c0 + (c + 1) * dk] = (a * nrm).astype(o_ref.dtype)

    y = _dot(h_scr[...], wab_ref[...])
    a = y + dtb_ref[...]
    softplus = jnp.maximum(a, 0.0) + jnp.log(1.0 + jnp.exp(-jnp.abs(a)))
    log_decay = -jnp.exp(alog_ref[...]) * softplus
    beta = jax.nn.sigmoid(y)
    lane = lax.broadcasted_iota(jnp.int32, y.shape, 1)
    bg_ref[...] = jnp.where(lane < heads, log_decay, beta)


def _gdn_proj(x2, gain, w_main, w_ab, conv_w, a_log_row, dt_bias_row, *, seq, heads, dk, tm=512):
    t, d = x2.shape
    n = w_main.shape[1]
    n_conv = conv_w.shape[1]
    assert t % tm == 0 and seq % tm == 0 and n % PROJ_TILE == 0 and n_conv % PROJ_TILE == 0
    assert PROJ_TILE % dk == 0 and (heads * dk) % PROJ_TILE == 0
    per_seq = seq // tm
    return pl.pallas_call(
        functools.partial(_gdn_proj_kernel, per_seq=per_seq, heads=heads, dk=dk),
        grid=(t // tm,),
        in_specs=[
            pl.BlockSpec((tm, d), lambda i: (i, 0)),
            _resident((1, d)),
            _resident((d, n)),
            _resident((d, LANES)),
            _resident((GDN_CONV, n_conv)),
            _resident((1, LANES)),
            _resident((1, LANES)),
        ],
        out_specs=[pl.BlockSpec((tm, n), lambda i: (i, 0)), pl.BlockSpec((tm, LANES), lambda i: (i, 0))],
        out_shape=[jax.ShapeDtypeStruct((t, n), BF16), jax.ShapeDtypeStruct((t, LANES), F32)],
        scratch_shapes=[
            pltpu.VMEM((tm, d), BF16),
            pltpu.VMEM((SUBLANES, n_conv), F32),
        ],
        compiler_params=_params("arbitrary"),
        name="gdn_proj",
    )(x2, gain.reshape(1, d), w_main, w_ab, conv_w, a_log_row, dt_bias_row)


def _gdn_kernel(q_ref, k_ref, v_ref, z_ref, bg_ref, gain_ref, o_ref, state, *, rows, chunk, heads, hp):
    hb = pl.program_id(1)
    c = pl.program_id(2)
    C = chunk
    nc = rows // C
    dk = q_ref.shape[1] // hp
    dv = v_ref.shape[1] // hp

    @pl.when(c == 0)
    def _():
        state[...] = jnp.zeros_like(state)

    lane = lax.broadcasted_iota(jnp.int32, (C, LANES), 1)
    ri = lax.broadcasted_iota(jnp.int32, (C, C), 0)
    ci = lax.broadcasted_iota(jnp.int32, (C, C), 1)
    tril = ci <= ri
    strict = ci < ri
    eye = jnp.where(ci == ri, 1.0, 0.0)
    ai = lax.broadcasted_iota(jnp.int32, (2 * C, C), 0)
    am = lax.broadcasted_iota(jnp.int32, (2 * C, C), 1)
    sum_sel = jnp.where(((ai < C) & (am <= ai)) | ((ai >= C) & (am > ai - C)), 1.0, 0.0).astype(BF16)
    bm = lax.broadcasted_iota(jnp.int32, (C, 2 * C), 0)
    bj = lax.broadcasted_iota(jnp.int32, (C, 2 * C), 1)
    col_sel = jnp.where((bm > bj) | (bj >= C), 1.0, 0.0)

    items = [(n, h) for n in range(nc) for h in range(hp)]
    r_of = lambda n: slice(n * C, (n + 1) * C)
    c_of = lambda h, w: slice(h * w, (h + 1) * w)
    bgs = [bg_ref[r_of(n), :] for n in range(nc)]
    pick = lambda n, ln: jnp.sum(jnp.where(lane == ln, bgs[n], 0.0), axis=-1, keepdims=True)
    g_col = [pick(n, hb * hp + h) for n, h in items]
    b_col = [pick(n, hb * hp + h + heads) for n, h in items]

    gmat = [g * col_sel for g in g_col]
    g_hi = [m.astype(BF16) for m in gmat]
    g_lo = [(m - hi.astype(F32)).astype(BF16) for m, hi in zip(gmat, g_hi)]
    sums_hi = [_dot(sum_sel, hi) for hi in g_hi]
    sums_lo = [_dot(sum_sel, lo) for lo in g_lo]
    sums = [a + b for a, b in zip(sums_hi, sums_lo)]
    decay = [jnp.where(tril, jnp.exp(s[:C, :C]), 0.0) for s in sums]
    g_cum = [s[:C, C:C + 1] for s in sums]
    exp_g = [jnp.exp(g) for g in g_cum]
    exp_rem = [jnp.exp(s[C:, C:C + 1]) for s in sums]
    exp_last = [e[C - 1:C, :] for e in exp_g]

    q = [q_ref[r_of(n), c_of(h, dk)] for n, h in items]
    k = [k_ref[r_of(n), c_of(h, dk)] for n, h in items]
    kf = [x.astype(F32) for x in k]
    k_beta = [x * b for x, b in zip(kf, b_col)]
    kk = [_dot_nt(kb.astype(BF16), x) for kb, x in zip(k_beta, k)]
    n_mat = [jnp.where(strict, a * d, 0.0) for a, d in zip(kk, decay)]
    inv = [eye - m for m in n_mat]
    n_bf = [m.astype(BF16) for m in n_mat]
    power = [_dot(m, m) for m in n_bf]
    inv = [x + _dot(x.astype(BF16), p.astype(BF16)) for x, p in zip(inv, power)]
    for _ in range(4):
        power = [_dot(p.astype(BF16), p.astype(BF16)) for p in power]
        inv = [x + _dot(x.astype(BF16), p.astype(BF16)) for x, p in zip(inv, power)]

    rhs = [jnp.concatenate([v_ref[r_of(n), c_of(h, dv)].astype(F32) * b, kb * eg], axis=1).astype(BF16)
           for (n, h), b, kb, eg in zip(items, b_col, k_beta, exp_g)]
    uw = [_dot(x.astype(BF16), r) for x, r in zip(inv, rhs)]
    qk = [_dot_nt(a, b) for a, b in zip(q, k)]
    attn = [jnp.where(tril, a * d, 0.0).astype(BF16) for a, d in zip(qk, decay)]
    kd = [(x * e).astype(BF16) for x, e in zip(kf, exp_rem)]

    st = [state[h] for h in range(hp)]
    for n in range(nc):
        idx = [n * hp + h for h in range(hp)]
        wq = [jnp.concatenate([uw[i][:, dv:].astype(BF16), q[i]], axis=0) for i in idx]
        ws = [_dot(a, s.astype(BF16)) for a, s in zip(wq, st)]
        v_new = [(uw[i][:, :dv] - w[:C]).astype(BF16) for i, w in zip(idx, ws)]
        o = [exp_g[i] * w[C:] + _dot(attn[i], vn) for i, w, vn in zip(idx, ws, v_new)]
        st = [s * exp_last[i] + _dot_tn(kd[i], vn) for i, s, vn in zip(idx, st, v_new)]
        for h in range(hp):
            gate = z_ref[r_of(n), c_of(h, dv)].astype(F32)
            o_ref[r_of(n), c_of(h, dv)] = (_rms(o[h], gain_ref[...]) * _silu(gate)).astype(o_ref.dtype)
    for h in range(hp):
        state[h] = st[h]


def _gdn_core(proj, bg, norm_gain, *, heads, dk, dv, rows=256, hp=8):
    b, s, _ = proj.shape
    assert s % rows == 0 and rows % GDN_CHUNK == 0 and dv == 2 * dk and GDN_CHUNK == 64
    assert heads % hp == 0
    nhb = heads // hp
    return pl.pallas_call(
        functools.partial(_gdn_kernel, rows=rows, chunk=GDN_CHUNK, heads=heads, hp=hp),
        grid=(b, nhb, s // rows),
        in_specs=[
            pl.BlockSpec((None, rows, hp * dk), lambda bi, h, c: (bi, c, h)),
            pl.BlockSpec((None, rows, hp * dk), lambda bi, h, c: (bi, c, nhb + h)),
            pl.BlockSpec((None, rows, hp * dv), lambda bi, h, c: (bi, c, nhb + h)),
            pl.BlockSpec((None, rows, hp * dv), lambda bi, h, c: (bi, c, 2 * nhb + h)),
            pl.BlockSpec((None, rows, LANES), lambda bi, h, c: (bi, c, 0)),
            pl.BlockSpec((1, dv), lambda bi, h, c: (0, 0)),
        ],
        out_specs=pl.BlockSpec((None, rows, hp * dv), lambda bi, h, c: (bi, c, h)),
        out_shape=jax.ShapeDtypeStruct((b, s, heads * dv), BF16),
        scratch_shapes=[pltpu.VMEM((hp, dk, dv), F32)],
        compiler_params=_params("parallel", "parallel", "arbitrary"),
        name="gdn_core",
    )(proj, proj, proj, proj, bg, norm_gain.reshape(1, dv))


def _sb_layer(x, gain, w_qkv, q_gain, k_gain, w_o):
    b, s, d = x.shape
    heads = w_qkv.shape[1] // (3 * SB_HEAD_DIM)
    x2 = x.reshape(b * s, d)
    qkv = _sb_proj(x2, gain, w_qkv.astype(BF16), q_gain, k_gain)
    o = _sb_attn(qkv.reshape(b, s, -1), heads=heads)
    return _oproj(o.reshape(b * s, -1), w_o.astype(BF16), x2).reshape(b, s, d)


def _ret_layer(x, gain, w_in, gn_gain, w_o):
    b, s, d = x.shape
    heads = RET_HEADS
    dv = gn_gain.shape[0]
    dk = dv // 2
    half = dk // 2
    inv = ROPE_BASE ** (-jnp.arange(half, dtype=F32) / half)
    ang = jnp.arange(s, dtype=F32)[:, None] * inv[None, :]
    log_gamma = jnp.log1p(-jnp.exp2(-5.0 - jnp.arange(heads, dtype=F32)))
    lg_rows = jnp.broadcast_to(log_gamma[:, None, None], (heads, 1, LANES))
    x2 = x.reshape(b * s, d)
    proj = _ret_proj(x2, gain, w_in.astype(BF16), jnp.cos(ang), jnp.sin(ang), seq=s, dk=dk, heads=heads)
    o = _ret_core(proj.reshape(b, s, -1), lg_rows, gn_gain, heads=heads, dk=dk, dv=dv)
    return _oproj(o.reshape(b * s, -1), w_o.astype(BF16), x2).reshape(b, s, d)


def _gdn_layer(x, gain, w_in, conv_w, a_log, dt_bias, norm_gain, w_o):
    b, s, d = x.shape
    heads = a_log.shape[0]
    dk = GDN_K_DIM
    dv = norm_gain.shape[0]
    n_main = 2 * heads * dk + 2 * heads * dv
    x2 = x.reshape(b * s, d)
    w_ab = jnp.zeros((d, LANES), BF16).at[:, :2 * heads].set(w_in[:, n_main:].astype(BF16))
    pad = lambda p: jnp.zeros((1, LANES), F32).at[0, :heads].set(p)
    proj, bg = _gdn_proj(x2, gain, w_in[:, :n_main].astype(BF16), w_ab, conv_w, pad(a_log), pad(dt_bias),
                         seq=s, heads=heads, dk=dk)
    o = _gdn_core(proj.reshape(b, s, -1), bg.reshape(b, s, LANES), norm_gain, heads=heads, dk=dk, dv=dv)
    return _oproj(o.reshape(b * s, -1), w_o.astype(BF16), x2).reshape(b, s, d)


def _ffn_layer(x, gain, w_in, w_out):
    b, s, d = x.shape
    return _ffn(x.reshape(b * s, d), gain, w_in.astype(BF16), w_out.astype(BF16)).reshape(b, s, d)


def kernel(x, norm_mix, norm_ffn, ffn_w_in, ffn_w_out, pool_w, pool_scale, sb_w_qkv, sb_q_gain, sb_k_gain, sb_w_o, ret_w_in, ret_gn_gain, ret_w_o, gdn_w_in, gdn_conv_w, gdn_a_log, gdn_dt_bias, gdn_norm_gain, gdn_w_o):
    depth = norm_mix.shape[0]
    for layer in range(depth):
        mixer, occ = layer % 4, layer // 4
        if mixer == 0:
            x = _pool_layer(x, norm_mix[layer], pool_w[occ].astype(BF16), pool_scale[occ])
        elif mixer == 1:
            x = _sb_layer(x, norm_mix[layer], sb_w_qkv[occ], sb_q_gain[occ], sb_k_gain[occ], sb_w_o[occ])
        elif mixer == 2:
            x = _ret_layer(x, norm_mix[layer], ret_w_in[occ], ret_gn_gain[occ], ret_w_o[occ])
        else:
            x = _gdn_layer(x, norm_mix[layer], gdn_w_in[occ], gdn_conv_w[occ], gdn_a_log[occ],
                           gdn_dt_bias[occ], gdn_norm_gain[occ], gdn_w_o[occ])
        x = _ffn_layer(x, norm_ffn[layer], ffn_w_in[layer], ffn_w_out[layer])
    return x
```

```python
import functools

import jax
import jax.numpy as jnp
from jax import lax
from jax.experimental import pallas as pl
from jax.experimental.pallas import tpu as pltpu

F32 = jnp.float32
BF16 = jnp.bfloat16

EPS = 1e-6
LOG2E = 1.4426950408889634
SOFTPLUS_LINEAR = 64.0
ROPE_BASE = 10000.0
LANES = 128
SUBLANES = 8
VMEM_LIMIT = 56 * 1024 * 1024
PROJ_TILE = 256

POOL_WINDOWS = (2, 4, 8, 16)
POOL_HALO = 32
SB_HEAD_DIM = 128
RET_HEADS = 4
RET_CHUNK = 128
GDN_K_DIM = 128
GDN_CONV = 4
GDN_CHUNK = 64
CONV_ROWS = 64


def _params(*sem):
    return pltpu.CompilerParams(dimension_semantics=sem, vmem_limit_bytes=VMEM_LIMIT)


def _resident(shape):
    return pl.BlockSpec(shape, lambda *_: (0,) * len(shape), pipeline_mode=pl.Buffered(1))


def _rms(xf, gain_row):
    ms = jnp.mean(xf * xf, axis=-1, keepdims=True)
    return xf * lax.rsqrt(ms + EPS) * gain_row


def _silu(x):
    return x * jax.nn.sigmoid(x)


def _dot(a, b):
    return jnp.dot(a, b, preferred_element_type=F32)


def _dot_nt(a, b):
    return lax.dot_general(a, b, (((1,), (1,)), ((), ())), preferred_element_type=F32)


def _dot_tn(a, b):
    return lax.dot_general(a, b, (((0,), (0,)), ((), ())), preferred_element_type=F32)


def _ffn_kernel(*refs, mixer_proj):
    if mixer_proj:
        x_ref, a_ref, wo_ref, g_ref, wi_ref, wd_ref, o_ref, h_scr, act_scr, x1_scr = refs
        x1_scr[...] = x_ref[...] + _dot(a_ref[...], wo_ref[...])
        x1_ref = x1_scr
    else:
        x_ref, g_ref, wi_ref, wd_ref, o_ref, h_scr, act_scr = refs
        x1_ref = x_ref
    h_scr[...] = _rms(x1_ref[...], g_ref[...]).astype(BF16)
    hidden = wd_ref.shape[0]
    for f0 in range(0, hidden, PROJ_TILE):
        gate = _dot(h_scr[...], wi_ref[:, f0:f0 + PROJ_TILE])
        up = _dot(h_scr[...], wi_ref[:, hidden + f0:hidden + f0 + PROJ_TILE])
        act_scr[:, f0:f0 + PROJ_TILE] = (_silu(gate) * up).astype(BF16)
    for c0 in range(0, o_ref.shape[1], PROJ_TILE):
        cols = slice(c0, c0 + PROJ_TILE)
        o_ref[:, cols] = x1_ref[:, cols] + _dot(act_scr[...], wd_ref[:, cols])


def _ffn(x2, gain, w_in, w_out, a2=None, w_o=None, *, tm=512):
    t, d = x2.shape
    hidden = w_out.shape[0]
    assert t % tm == 0 and hidden % PROJ_TILE == 0 and d % PROJ_TILE == 0
    row = lambda width: pl.BlockSpec((tm, width), lambda i: (i, 0))
    mixer_proj = a2 is not None
    in_specs = [row(d)]
    args = [x2]
    scratch = [pltpu.VMEM((tm, d), BF16), pltpu.VMEM((tm, hidden), BF16)]
    if mixer_proj:
        in_specs += [row(a2.shape[1]), _resident(w_o.shape)]
        args += [a2, w_o]
        scratch.append(pltpu.VMEM((tm, d), F32))
    in_specs += [_resident((1, d)), _resident(w_in.shape), _resident(w_out.shape)]
    args += [gain.reshape(1, d), w_in, w_out]
    return pl.pallas_call(
        functools.partial(_ffn_kernel, mixer_proj=mixer_proj),
        grid=(t // tm,),
        in_specs=in_specs,
        out_specs=row(d),
        out_shape=jax.ShapeDtypeStruct((t, d), F32),
        scratch_shapes=scratch,
        compiler_params=_params("parallel"),
        name="ffn",
    )(*args)


def _pool_kernel(x_ref, g_ref, w_ref, sc_ref, o_ref, ext, buf_a, buf_b, *, ts, group):
    s = pl.program_id(1)
    n = ts + POOL_HALO
    d = x_ref.shape[-1]

    @pl.when(s == 0)
    def _():
        ext[0:POOL_HALO, :] = jnp.zeros((POOL_HALO, d), F32)

    x = x_ref[...]
    ext[POOL_HALO:n, :] = _rms(x, g_ref[...])

    src, dst = ext, buf_a
    for level in range(1, len(POOL_WINDOWS) + 1):
        shift = 1 << (level - 1)
        lo = SUBLANES * level
        c0 = (level - 1) * group
        dst[lo:n, c0:d] = src[lo:n, c0:d] + src[lo - shift:n - shift, c0:d]
        src, dst = dst, (buf_b if dst is buf_a else buf_a)

    pos1 = (s * ts + lax.broadcasted_iota(jnp.int32, (ts, 1), 0) + 1).astype(F32)
    for g, win in enumerate(POOL_WINDOWS):
        level_buf = buf_a if g % 2 == 0 else buf_b
        cols = slice(g * group, (g + 1) * group)
        inv_cnt = 1.0 / jnp.minimum(pos1, float(win))
        pooled = level_buf[POOL_HALO:n, cols] * inv_cnt - ext[POOL_HALO:n, cols]
        mixed = _dot(pooled.astype(BF16), w_ref[g])
        o_ref[:, cols] = x[:, cols] + mixed * sc_ref[:, cols]

    ext[0:POOL_HALO, :] = ext[ts:n, :]


def _pool_layer(x, gain, w_group, scale, *, ts=512):
    b, s, d = x.shape
    ngroups, group, _ = w_group.shape
    assert s % ts == 0 and ngroups == len(POOL_WINDOWS) and ngroups * group == d
    n = ts + POOL_HALO
    return pl.pallas_call(
        functools.partial(_pool_kernel, ts=ts, group=group),
        grid=(b, s // ts),
        in_specs=[
            pl.BlockSpec((None, ts, d), lambda i, j: (i, j, 0)),
            pl.BlockSpec((1, d), lambda i, j: (0, 0)),
            pl.BlockSpec((ngroups, group, group), lambda i, j: (0, 0, 0)),
            pl.BlockSpec((1, d), lambda i, j: (0, 0)),
        ],
        out_specs=pl.BlockSpec((None, ts, d), lambda i, j: (i, j, 0)),
        out_shape=jax.ShapeDtypeStruct((b, s, d), F32),
        scratch_shapes=[pltpu.VMEM((n, d), F32)] * 3,
        compiler_params=_params("arbitrary", "arbitrary"),
        name="pool_layer",
    )(x, gain.reshape(1, d), w_group, scale.reshape(1, d))


def _sb_proj_kernel(x_ref, g_ref, w_ref, qg_ref, kg_ref, o_ref, h_scr, *, width):
    h_scr[...] = _rms(x_ref[...], g_ref[...]).astype(BF16)
    gains = (qg_ref[...] * (SB_HEAD_DIM ** -0.5 * LOG2E), kg_ref[...], None)
    for tile in range(w_ref.shape[1] // PROJ_TILE):
        cols = slice(tile * PROJ_TILE, (tile + 1) * PROJ_TILE)
        y = _dot(h_scr[...], w_ref[:, cols])
        gain = gains[tile * PROJ_TILE // width]
        if gain is None:
            o_ref[:, cols] = y.astype(o_ref.dtype)
        else:
            for c in range(PROJ_TILE // SB_HEAD_DIM):
                sub = slice(c * SB_HEAD_DIM, (c + 1) * SB_HEAD_DIM)
                o_ref[:, tile * PROJ_TILE + c * SB_HEAD_DIM:tile * PROJ_TILE + (c + 1) * SB_HEAD_DIM] = (
                    _rms(y[:, sub], gain).astype(o_ref.dtype))


def _sb_proj(x2, gain, w_qkv, q_gain, k_gain, *, tm=512):
    t, d = x2.shape
    n = w_qkv.shape[1]
    width = n // 3
    assert t % tm == 0 and width % PROJ_TILE == 0 and PROJ_TILE % SB_HEAD_DIM == 0
    return pl.pallas_call(
        functools.partial(_sb_proj_kernel, width=width),
        grid=(t // tm,),
        in_specs=[
            pl.BlockSpec((tm, d), lambda i: (i, 0)),
            _resident((1, d)),
            _resident((d, n)),
            _resident((1, SB_HEAD_DIM)),
            _resident((1, SB_HEAD_DIM)),
        ],
        out_specs=pl.BlockSpec((tm, n), lambda i: (i, 0)),
        out_shape=jax.ShapeDtypeStruct((t, n), BF16),
        scratch_shapes=[pltpu.VMEM((tm, d), BF16)],
        compiler_params=_params("parallel"),
        name="sb_proj",
    )(x2, gain.reshape(1, d), w_qkv, q_gain.reshape(1, -1), k_gain.reshape(1, -1))


def _sb_attn_kernel(q_ref, k_ref, v_ref, o_ref, *, tq, unroll):
    i = pl.program_id(2)
    tk = tq
    q = q_ref[...]
    row = lax.broadcasted_iota(jnp.int32, (tq, tk), 0)
    col = lax.broadcasted_iota(jnp.int32, (tq, tk), 1)
    causal = col < row
    jj = lax.broadcasted_iota(jnp.int32, (tk, tk), 0)
    ss = lax.broadcasted_iota(jnp.int32, (tk, tk), 1)
    suffix = jnp.where(jj > ss, 1.0, 0.0).astype(BF16)

    def scores(kb, masked):
        start = pl.multiple_of(kb * tk, tk)
        z = _dot_nt(q, k_ref[pl.ds(start, tk), :])
        p = jnp.where(z > SOFTPLUS_LINEAR, z, jnp.log(1.0 + jnp.exp2(z)) * LOG2E)
        log_beta = z - p
        if masked:
            p = jnp.where(causal, p, 0.0)
        return p.astype(BF16), log_beta, p[:, :1]

    def blocks(kbs, masks, acc, run):
        stage1 = [scores(kb, masked) for kb, masked in zip(kbs, masks)]
        afters = [_dot(p_bf, suffix) for p_bf, _, _ in stage1]
        for kb, masked, (_, log_beta, first_col), after in zip(kbs, masks, stage1, afters):
            wts = jnp.exp2(log_beta - (after + run))
            if masked:
                wts = jnp.where(causal, wts, 0.0)
            start = pl.multiple_of(kb * tk, tk)
            acc = acc + _dot(wts.astype(BF16), v_ref[pl.ds(start, tk), :])
            run = run + (after[:, :1] + first_col)
        return acc, run

    rem = i % unroll

    def head(n_extra):
        def branch():
            acc = jnp.zeros((tq, q.shape[1]), F32)
            run = jnp.zeros((tq, 1), F32)
            kbs = [i] + [i - 1 - u for u in range(n_extra)]
            return blocks(kbs, [True] + [False] * n_extra, acc, run)
        return branch

    def group(t, carry):
        first = i - 1 - rem - unroll * t
        return blocks([first - u for u in range(unroll)], [False] * unroll, carry[0], carry[1])

    carry = lax.switch(rem, [head(n) for n in range(unroll)])
    acc, run = lax.fori_loop(0, i // unroll, group, carry)
    o_ref[...] = acc.astype(o_ref.dtype)


def _sb_attn(qkv, *, heads, tq=256, unroll=8):
    b, s, _ = qkv.shape
    dh = SB_HEAD_DIM
    assert s % tq == 0
    return pl.pallas_call(
        functools.partial(_sb_attn_kernel, tq=tq, unroll=unroll),
        grid=(b, heads, s // tq),
        in_specs=[
            pl.BlockSpec((None, tq, dh), lambda bi, h, i: (bi, i, h)),
            pl.BlockSpec((None, s, dh), lambda bi, h, i: (bi, 0, heads + h)),
            pl.BlockSpec((None, s, dh), lambda bi, h, i: (bi, 0, 2 * heads + h)),
        ],
        out_specs=pl.BlockSpec((None, tq, dh), lambda bi, h, i: (bi, i, h)),
        out_shape=jax.ShapeDtypeStruct((b, s, heads * dh), BF16),
        compiler_params=_params("parallel", "parallel", "arbitrary"),
        name="sb_attn",
    )(qkv, qkv, qkv)


def _ret_proj_kernel(x_ref, g_ref, w_ref, cos_ref, sin_ref, o_ref, h_scr, *, dk, heads):
    h_scr[...] = _rms(x_ref[...], g_ref[...]).astype(BF16)
    half = dk // 2
    cos = cos_ref[...]
    sin = sin_ref[...]
    for tile in range(w_ref.shape[1] // dk):
        c0 = tile * dk
        y = _dot(h_scr[...], w_ref[:, c0:c0 + dk])
        if tile < 2 * heads:
            if tile >= heads:
                y = y * (dk ** -0.5)
            x1 = y[:, :half]
            x2 = y[:, half:]
            o_ref[:, c0:c0 + half] = (x1 * cos - x2 * sin).astype(o_ref.dtype)
            o_ref[:, c0 + half:c0 + dk] = (x1 * sin + x2 * cos).astype(o_ref.dtype)
        else:
            o_ref[:, c0:c0 + dk] = y.astype(o_ref.dtype)


def _ret_proj(x2, gain, w_in, cos, sin, *, seq, dk, heads, tm=512):
    t, d = x2.shape
    n = w_in.shape[1]
    assert t % tm == 0 and seq % tm == 0 and n % dk == 0 and dk == PROJ_TILE
    per_seq = seq // tm
    return pl.pallas_call(
        functools.partial(_ret_proj_kernel, dk=dk, heads=heads),
        grid=(t // tm,),
        in_specs=[
            pl.BlockSpec((tm, d), lambda i: (i, 0)),
            _resident((1, d)),
            _resident((d, n)),
            pl.BlockSpec((tm, dk // 2), lambda i: (i % per_seq, 0)),
            pl.BlockSpec((tm, dk // 2), lambda i: (i % per_seq, 0)),
        ],
        out_specs=pl.BlockSpec((tm, n), lambda i: (i, 0)),
        out_shape=jax.ShapeDtypeStruct((t, n), BF16),
        scratch_shapes=[pltpu.VMEM((tm, d), BF16)],
        compiler_params=_params("parallel"),
        name="ret_proj",
    )(x2, gain.reshape(1, d), w_in, cos, sin)


def _ret_kernel(lg_ref, q_ref, k_ref, v_ref, g_ref, gn_ref, o_ref, state, *, rows, chunk, heads):
    c = pl.program_id(1)
    dk = q_ref.shape[1] // heads
    dv = v_ref.shape[1] // heads

    @pl.when(c == 0)
    def _():
        state[...] = jnp.zeros_like(state)

    ri = lax.broadcasted_iota(jnp.int32, (chunk, chunk), 0)
    ci = lax.broadcasted_iota(jnp.int32, (chunk, chunk), 1)
    diff = (ri - ci).astype(F32)
    idx = lax.broadcasted_iota(jnp.int32, (chunk, 1), 0).astype(F32)
    lg = [lg_ref[h][:, :1] for h in range(heads)]
    intra = [jnp.where(diff >= 0, jnp.exp(jnp.maximum(diff, 0.0) * l), 0.0) for l in lg]
    q_dec = [jnp.exp((idx + 1.0) * l) for l in lg]
    k_dec = [jnp.exp((chunk - 1.0 - idx) * l) for l in lg]
    chunk_dec = [jnp.exp(chunk * l) for l in lg]

    st = [state[h] for h in range(heads)]
    for n in range(rows // chunk):
        r = slice(n * chunk, (n + 1) * chunk)
        q = [q_ref[r, h * dk:(h + 1) * dk] for h in range(heads)]
        k = [k_ref[r, h * dk:(h + 1) * dk] for h in range(heads)]
        v = [v_ref[r, h * dv:(h + 1) * dv] for h in range(heads)]
        scores = [(_dot_nt(a, b) * m).astype(BF16) for a, b, m in zip(q, k, intra)]
        kd = [(b.astype(F32) * d).astype(BF16) for b, d in zip(k, k_dec)]
        inter = [_dot(a, s.astype(BF16)) for a, s in zip(q, st)]
        kv = [_dot_tn(a, b) for a, b in zip(kd, v)]
        o = [_dot(sc, b) + d * it for sc, b, d, it in zip(scores, v, q_dec, inter)]
        st = [s * d + u for s, d, u in zip(st, chunk_dec, kv)]
        for h in range(heads):
            gate = g_ref[r, h * dv:(h + 1) * dv].astype(F32)
            o_ref[r, h * dv:(h + 1) * dv] = (_rms(o[h], gn_ref[...]) * _silu(gate)).astype(o_ref.dtype)
    for h in range(heads):
        state[h] = st[h]


def _ret_core(proj, lg_rows, gn_gain, *, heads, dk, dv, rows=512):
    b, s, _ = proj.shape
    assert s % rows == 0 and rows % RET_CHUNK == 0 and dv == 2 * dk
    return pl.pallas_call(
        functools.partial(_ret_kernel, rows=rows, chunk=RET_CHUNK, heads=heads),
        grid=(b, s // rows),
        in_specs=[
            _resident((heads, 1, LANES)),
            pl.BlockSpec((None, rows, heads * dk), lambda bi, c: (bi, c, 0)),
            pl.BlockSpec((None, rows, heads * dk), lambda bi, c: (bi, c, 1)),
            pl.BlockSpec((None, rows, heads * dv), lambda bi, c: (bi, c, 1)),
            pl.BlockSpec((None, rows, heads * dv), lambda bi, c: (bi, c, 2)),
            _resident((1, dv)),
        ],
        out_specs=pl.BlockSpec((None, rows, heads * dv), lambda bi, c: (bi, c, 0)),
        out_shape=jax.ShapeDtypeStruct((b, s, heads * dv), BF16),
        scratch_shapes=[pltpu.VMEM((heads, dk, dv), F32)],
        compiler_params=_params("parallel", "arbitrary"),
        name="ret_core",
    )(lg_rows, proj, proj, proj, proj, gn_gain.reshape(1, dv))


def _gdn_proj_kernel(x_ref, g_ref, w_ref, wab_ref, cw_ref, alog_ref, dtb_ref, o_ref, bg_ref,
                     h_scr, halo, *, per_seq, heads, dk):
    i = pl.program_id(0)
    tm = x_ref.shape[0]
    n_conv = cw_ref.shape[1]

    @pl.when(i == 0)
    def _():
        halo[...] = jnp.zeros_like(halo)

    h_scr[...] = _rms(x_ref[...], g_ref[...]).astype(BF16)
    first = (i % per_seq) == 0
    row8 = lax.broadcasted_iota(jnp.int32, (SUBLANES, PROJ_TILE), 0)
    for tile in range(w_ref.shape[1] // PROJ_TILE):
        c0 = tile * PROJ_TILE
        cols = slice(c0, c0 + PROJ_TILE)
        y = _dot(h_scr[...], w_ref[:, cols])
        if c0 >= n_conv:
            o_ref[:, cols] = y.astype(o_ref.dtype)
            continue
        prev = jnp.where(first, 0.0, halo[:, cols])
        halo[:, cols] = y[tm - SUBLANES:tm, :]
        for r0 in range(0, tm, CONV_ROWS):
            rows = slice(r0, r0 + CONV_ROWS)
            yc = y[rows]
            conv = cw_ref[GDN_CONV - 1:GDN_CONV, cols] * yc
            for back in range(1, GDN_CONV):
                shifted = pltpu.roll(yc, back, axis=0)
                top = jnp.where(row8 < back, pltpu.roll(prev, back, axis=0), shifted[:SUBLANES])
                shifted = jnp.concatenate([top, shifted[SUBLANES:]], axis=0)
                conv = conv + cw_ref[GDN_CONV - 1 - back:GDN_CONV - back, cols] * shifted
            prev = yc[CONV_ROWS - SUBLANES:]
            act = _silu(conv)
            if c0 >= 2 * heads * dk:
                o_ref[rows, cols] = act.astype(o_ref.dtype)
                continue
            scale = dk ** -0.5 if c0 < heads * dk else 1.0
            for c in range(PROJ_TILE // dk):
                a = act[:, c * dk:(c + 1) * dk]
                nrm = lax.rsqrt(jnp.sum(a * a, axis=-1, keepdims=True) + EPS) * scale
                o_ref[rows, c0 + c * dk:c0 + (c + 1) * dk] = (a * nrm).astype(o_ref.dtype)

    y = _dot(h_scr[...], wab_ref[...])
    a = y + dtb_ref[...]
    softplus = jnp.maximum(a, 0.0) + jnp.log(1.0 + jnp.exp(-jnp.abs(a)))
    log_decay = -jnp.exp(alog_ref[...]) * softplus
    beta = jax.nn.sigmoid(y)
    lane = lax.broadcasted_iota(jnp.int32, y.shape, 1)
    bg_ref[...] = jnp.where(lane < heads, log_decay, beta)


def _gdn_proj(x2, gain, w_main, w_ab, conv_w, a_log_row, dt_bias_row, *, seq, heads, dk, tm=512):
    t, d = x2.shape
    n = w_main.shape[1]
    n_conv = conv_w.shape[1]
    assert t % tm == 0 and seq % tm == 0 and n % PROJ_TILE == 0 and n_conv % PROJ_TILE == 0
    assert PROJ_TILE % dk == 0 and (heads * dk) % PROJ_TILE == 0
    per_seq = seq // tm
    return pl.pallas_call(
        functools.partial(_gdn_proj_kernel, per_seq=per_seq, heads=heads, dk=dk),
        grid=(t // tm,),
        in_specs=[
            pl.BlockSpec((tm, d), lambda i: (i, 0)),
            _resident((1, d)),
            _resident((d, n)),
            _resident((d, LANES)),
            _resident((GDN_CONV, n_conv)),
            _resident((1, LANES)),
            _resident((1, LANES)),
        ],
        out_specs=[pl.BlockSpec((tm, n), lambda i: (i, 0)), pl.BlockSpec((tm, LANES), lambda i: (i, 0))],
        out_shape=[jax.ShapeDtypeStruct((t, n), BF16), jax.ShapeDtypeStruct((t, LANES), F32)],
        scratch_shapes=[
            pltpu.VMEM((tm, d), BF16),
            pltpu.VMEM((SUBLANES, n_conv), F32),
        ],
        compiler_params=_params("arbitrary"),
        name="gdn_proj",
    )(x2, gain.reshape(1, d), w_main, w_ab, conv_w, a_log_row, dt_bias_row)


def _gdn_kernel(q_ref, k_ref, v_ref, z_ref, bg_ref, gain_ref, o_ref, state, *, rows, chunk, heads, hp):
    hb = pl.program_id(1)
    c = pl.program_id(2)
    C = chunk
    nc = rows // C
    dk = q_ref.shape[1] // hp
    dv = v_ref.shape[1] // hp

    @pl.when(c == 0)
    def _():
        state[...] = jnp.zeros_like(state)

    lane = lax.broadcasted_iota(jnp.int32, (C, LANES), 1)
    ri = lax.broadcasted_iota(jnp.int32, (C, C), 0)
    ci = lax.broadcasted_iota(jnp.int32, (C, C), 1)
    tril = ci <= ri
    strict = ci < ri
    eye = jnp.where(ci == ri, 1.0, 0.0)
    ai = lax.broadcasted_iota(jnp.int32, (2 * C, C), 0)
    am = lax.broadcasted_iota(jnp.int32, (2 * C, C), 1)
    sum_sel = jnp.where(((ai < C) & (am <= ai)) | ((ai >= C) & (am > ai - C)), 1.0, 0.0).astype(BF16)
    sum_sel2 = jnp.concatenate([sum_sel, sum_sel], axis=1)
    bm = lax.broadcasted_iota(jnp.int32, (C, 2 * C), 0)
    bj = lax.broadcasted_iota(jnp.int32, (C, 2 * C), 1)
    col_sel = jnp.where((bm > bj) | (bj >= C), 1.0, 0.0)

    items = [(n, h) for n in range(nc) for h in range(hp)]
    r_of = lambda n: slice(n * C, (n + 1) * C)
    c_of = lambda h, w: slice(h * w, (h + 1) * w)
    bgs = [bg_ref[r_of(n), :] for n in range(nc)]
    pick = lambda n, ln: jnp.sum(jnp.where(lane == ln, bgs[n], 0.0), axis=-1, keepdims=True)
    g_col = [pick(n, hb * hp + h) for n, h in items]
    b_col = [pick(n, hb * hp + h + heads) for n, h in items]

    gmat = [g * col_sel for g in g_col]
    g_hi = [m.astype(BF16) for m in gmat]
    g_split = [jnp.concatenate([hi, (m - hi.astype(F32)).astype(BF16)], axis=0) for m, hi in zip(gmat, g_hi)]
    sums = [_dot(sum_sel2, m) for m in g_split]
    decay = [jnp.where(tril, jnp.exp(s[:C, :C]), 0.0) for s in sums]
    g_cum = [s[:C, C:C + 1] for s in sums]
    exp_g = [jnp.exp(g) for g in g_cum]
    exp_rem = [jnp.exp(s[C:, C:C + 1]) for s in sums]
    exp_last = [e[C - 1:C, :] for e in exp_g]

    q = [q_ref[r_of(n), c_of(h, dk)] for n, h in items]
    k = [k_ref[r_of(n), c_of(h, dk)] for n, h in items]
    kf = [x.astype(F32) for x in k]
    k_beta = [x * b for x, b in zip(kf, b_col)]
    kq = [_dot_nt(jnp.concatenate([kb.astype(BF16), a], axis=0), x) for kb, a, x in zip(k_beta, q, k)]
    n_mat = [jnp.where(strict, m[:C] * d, 0.0) for m, d in zip(kq, decay)]
    attn = [jnp.where(tril, m[C:] * d, 0.0).astype(BF16) for m, d in zip(kq, decay)]
    inv = [eye - m for m in n_mat]
    n_bf = [m.astype(BF16) for m in n_mat]
    power = [_dot(m, m) for m in n_bf]
    for _ in range(4):
        p_bf = [p.astype(BF16) for p in power]
        both = [_dot(jnp.concatenate([x.astype(BF16), p], axis=0), p) for x, p in zip(inv, p_bf)]
        inv = [x + m[:C] for x, m in zip(inv, both)]
        power = [m[C:] for m in both]
    inv = [x + _dot(x.astype(BF16), p.astype(BF16)) for x, p in zip(inv, power)]

    rhs = [jnp.concatenate([v_ref[r_of(n), c_of(h, dv)].astype(F32) * b, kb * eg], axis=1).astype(BF16)
           for (n, h), b, kb, eg in zip(items, b_col, k_beta, exp_g)]
    uw = [_dot(x.astype(BF16), r) for x, r in zip(inv, rhs)]
    kd = [(x * e).astype(BF16) for x, e in zip(kf, exp_rem)]

    st = [state[h] for h in range(hp)]
    for n in range(nc):
        idx = [n * hp + h for h in range(hp)]
        wq = [jnp.concatenate([uw[i][:, dv:].astype(BF16), q[i]], axis=0) for i in idx]
        ws = [_dot(a, s.astype(BF16)) for a, s in zip(wq, st)]
        v_new = [(uw[i][:, :dv] - w[:C]).astype(BF16) for i, w in zip(idx, ws)]
        o = [exp_g[i] * w[C:] + _dot(attn[i], vn) for i, w, vn in zip(idx, ws, v_new)]
        st = [s * exp_last[i] + _dot_tn(kd[i], vn) for i, s, vn in zip(idx, st, v_new)]
        for h in range(hp):
            gate = z_ref[r_of(n), c_of(h, dv)].astype(F32)
            o_ref[r_of(n), c_of(h, dv)] = (_rms(o[h], gain_ref[...]) * _silu(gate)).astype(o_ref.dtype)
    for h in range(hp):
        state[h] = st[h]


def _gdn_core(proj, bg, norm_gain, *, heads, dk, dv, rows=256, hp=8):
    b, s, _ = proj.shape
    assert s % rows == 0 and rows % GDN_CHUNK == 0 and dv == 2 * dk and GDN_CHUNK == 64
    assert heads % hp == 0
    nhb = heads // hp
    return pl.pallas_call(
        functools.partial(_gdn_kernel, rows=rows, chunk=GDN_CHUNK, heads=heads, hp=hp),
        grid=(b, nhb, s // rows),
        in_specs=[
            pl.BlockSpec((None, rows, hp * dk), lambda bi, h, c: (bi, c, h)),
            pl.BlockSpec((None, rows, hp * dk), lambda bi, h, c: (bi, c, nhb + h)),
            pl.BlockSpec((None, rows, hp * dv), lambda bi, h, c: (bi, c, nhb + h)),
            pl.BlockSpec((None, rows, hp * dv), lambda bi, h, c: (bi, c, 2 * nhb + h)),
            pl.BlockSpec((None, rows, LANES), lambda bi, h, c: (bi, c, 0)),
            pl.BlockSpec((1, dv), lambda bi, h, c: (0, 0)),
        ],
        out_specs=pl.BlockSpec((None, rows, hp * dv), lambda bi, h, c: (bi, c, h)),
        out_shape=jax.ShapeDtypeStruct((b, s, heads * dv), BF16),
        scratch_shapes=[pltpu.VMEM((hp, dk, dv), F32)],
        compiler_params=_params("parallel", "parallel", "arbitrary"),
        name="gdn_core",
    )(proj, proj, proj, proj, bg, norm_gain.reshape(1, dv))


def _sb_mixer(x, gain, w_qkv, q_gain, k_gain):
    b, s, d = x.shape
    heads = w_qkv.shape[1] // (3 * SB_HEAD_DIM)
    qkv = _sb_proj(x.reshape(b * s, d), gain, w_qkv.astype(BF16), q_gain, k_gain)
    return _sb_attn(qkv.reshape(b, s, -1), heads=heads)


def _ret_mixer(x, gain, w_in, gn_gain):
    b, s, d = x.shape
    heads = RET_HEADS
    dv = gn_gain.shape[0]
    dk = dv // 2
    half = dk // 2
    inv = ROPE_BASE ** (-jnp.arange(half, dtype=F32) / half)
    ang = jnp.arange(s, dtype=F32)[:, None] * inv[None, :]
    log_gamma = jnp.log1p(-jnp.exp2(-5.0 - jnp.arange(heads, dtype=F32)))
    lg_rows = jnp.broadcast_to(log_gamma[:, None, None], (heads, 1, LANES))
    proj = _ret_proj(x.reshape(b * s, d), gain, w_in.astype(BF16), jnp.cos(ang), jnp.sin(ang),
                     seq=s, dk=dk, heads=heads)
    return _ret_core(proj.reshape(b, s, -1), lg_rows, gn_gain, heads=heads, dk=dk, dv=dv)


def _gdn_mixer(x, gain, w_in, conv_w, a_log, dt_bias, norm_gain):
    b, s, d = x.shape
    heads = a_log.shape[0]
    dk = GDN_K_DIM
    dv = norm_gain.shape[0]
    n_main = 2 * heads * dk + 2 * heads * dv
    w_ab = jnp.zeros((d, LANES), BF16).at[:, :2 * heads].set(w_in[:, n_main:].astype(BF16))
    pad = lambda p: jnp.zeros((1, LANES), F32).at[0, :heads].set(p)
    proj, bg = _gdn_proj(x.reshape(b * s, d), gain, w_in[:, :n_main].astype(BF16), w_ab, conv_w,
                         pad(a_log), pad(dt_bias), seq=s, heads=heads, dk=dk)
    return _gdn_core(proj.reshape(b, s, -1), bg.reshape(b, s, LANES), norm_gain, heads=heads, dk=dk, dv=dv)


def _ffn_layer(x, gain, w_in, w_out, o=None, w_o=None):
    b, s, d = x.shape
    if o is not None:
        o, w_o = o.reshape(b * s, -1), w_o.astype(BF16)
    return _ffn(x.reshape(b * s, d), gain, w_in.astype(BF16), w_out.astype(BF16), o, w_o).reshape(b, s, d)


def kernel(x, norm_mix, norm_ffn, ffn_w_in, ffn_w_out, pool_w, pool_scale, sb_w_qkv, sb_q_gain, sb_k_gain, sb_w_o, ret_w_in, ret_gn_gain, ret_w_o, gdn_w_in, gdn_conv_w, gdn_a_log, gdn_dt_bias, gdn_norm_gain, gdn_w_o):
    depth = norm_mix.shape[0]
    for layer in range(depth):
        mixer, occ = layer % 4, layer // 4
        o, w_o = None, None
        if mixer == 0:
            x = _pool_layer(x, norm_mix[layer], pool_w[occ].astype(BF16), pool_scale[occ])
        elif mixer == 1:
            o, w_o = _sb_mixer(x, norm_mix[layer], sb_w_qkv[occ], sb_q_gain[occ], sb_k_gain[occ]), sb_w_o[occ]
        elif mixer == 2:
            o, w_o = _ret_mixer(x, norm_mix[layer], ret_w_in[occ], ret_gn_gain[occ]), ret_w_o[occ]
        else:
            o, w_o = _gdn_mixer(x, norm_mix[layer], gdn_w_in[occ], gdn_conv_w[occ], gdn_a_log[occ],
                                gdn_dt_bias[occ], gdn_norm_gain[occ]), gdn_w_o[occ]
        x = _ffn_layer(x, norm_ffn[layer], ffn_w_in[layer], ffn_w_out[layer], o, w_o)
    return x
```

```python
import functools

import jax
import jax.numpy as jnp
from jax import lax
from jax.experimental import pallas as pl
from jax.experimental.pallas import tpu as pltpu

F32 = jnp.float32
BF16 = jnp.bfloat16

EPS = 1e-6
LOG2E = 1.4426950408889634
SOFTPLUS_LINEAR = 64.0
UNDERFLOW_LOG2 = 160.0
ROPE_BASE = 10000.0
LANES = 128
SUBLANES = 8
VMEM_LIMIT = 56 * 1024 * 1024
PROJ_TILE = 256

POOL_WINDOWS = (2, 4, 8, 16)
POOL_HALO = 32
SB_HEAD_DIM = 128
RET_HEADS = 4
RET_CHUNK = 128
GDN_K_DIM = 128
GDN_CONV = 4
GDN_CHUNK = 64
CONV_ROWS = 64


def _params(*sem):
    return pltpu.CompilerParams(dimension_semantics=sem, vmem_limit_bytes=VMEM_LIMIT)


def _resident(shape):
    return pl.BlockSpec(shape, lambda *_: (0,) * len(shape), pipeline_mode=pl.Buffered(1))


def _rms(xf, gain_row):
    ms = jnp.mean(xf * xf, axis=-1, keepdims=True)
    return xf * lax.rsqrt(ms + EPS) * gain_row


def _silu(x):
    return x * jax.nn.sigmoid(x)


def _dot(a, b):
    return jnp.dot(a, b, preferred_element_type=F32)


def _dot_nt(a, b):
    return lax.dot_general(a, b, (((1,), (1,)), ((), ())), preferred_element_type=F32)


def _dot_tn(a, b):
    return lax.dot_general(a, b, (((0,), (0,)), ((), ())), preferred_element_type=F32)


def _ffn_kernel(*refs, mixer_proj):
    if mixer_proj:
        x_ref, a_ref, wo_ref, g_ref, wi_ref, wd_ref, o_ref, h_scr, act_scr, x1_scr = refs
        x1_scr[...] = x_ref[...] + _dot(a_ref[...], wo_ref[...])
        x1_ref = x1_scr
    else:
        x_ref, g_ref, wi_ref, wd_ref, o_ref, h_scr, act_scr = refs
        x1_ref = x_ref
    h_scr[...] = _rms(x1_ref[...], g_ref[...]).astype(BF16)
    hidden = wd_ref.shape[0]
    for f0 in range(0, hidden, PROJ_TILE):
        gate = _dot(h_scr[...], wi_ref[:, f0:f0 + PROJ_TILE])
        up = _dot(h_scr[...], wi_ref[:, hidden + f0:hidden + f0 + PROJ_TILE])
        act_scr[:, f0:f0 + PROJ_TILE] = (_silu(gate) * up).astype(BF16)
    for c0 in range(0, o_ref.shape[1], PROJ_TILE):
        cols = slice(c0, c0 + PROJ_TILE)
        o_ref[:, cols] = x1_ref[:, cols] + _dot(act_scr[...], wd_ref[:, cols])


def _ffn(x2, gain, w_in, w_out, a2=None, w_o=None, *, tm=512):
    t, d = x2.shape
    hidden = w_out.shape[0]
    assert t % tm == 0 and hidden % PROJ_TILE == 0 and d % PROJ_TILE == 0
    row = lambda width: pl.BlockSpec((tm, width), lambda i: (i, 0))
    mixer_proj = a2 is not None
    in_specs = [row(d)]
    args = [x2]
    scratch = [pltpu.VMEM((tm, d), BF16), pltpu.VMEM((tm, hidden), BF16)]
    if mixer_proj:
        in_specs += [row(a2.shape[1]), _resident(w_o.shape)]
        args += [a2, w_o]
        scratch.append(pltpu.VMEM((tm, d), F32))
    in_specs += [_resident((1, d)), _resident(w_in.shape), _resident(w_out.shape)]
    args += [gain.reshape(1, d), w_in, w_out]
    return pl.pallas_call(
        functools.partial(_ffn_kernel, mixer_proj=mixer_proj),
        grid=(t // tm,),
        in_specs=in_specs,
        out_specs=row(d),
        out_shape=jax.ShapeDtypeStruct((t, d), F32),
        scratch_shapes=scratch,
        compiler_params=_params("parallel"),
        name="ffn",
    )(*args)


def _pool_kernel(x_ref, g_ref, w_ref, sc_ref, o_ref, ext, buf_a, buf_b, *, ts, group):
    s = pl.program_id(1)
    n = ts + POOL_HALO
    d = x_ref.shape[-1]

    @pl.when(s == 0)
    def _():
        ext[0:POOL_HALO, :] = jnp.zeros((POOL_HALO, d), F32)

    x = x_ref[...]
    ext[POOL_HALO:n, :] = _rms(x, g_ref[...])

    src, dst = ext, buf_a
    for level in range(1, len(POOL_WINDOWS) + 1):
        shift = 1 << (level - 1)
        lo = SUBLANES * level
        c0 = (level - 1) * group
        dst[lo:n, c0:d] = src[lo:n, c0:d] + src[lo - shift:n - shift, c0:d]
        src, dst = dst, (buf_b if dst is buf_a else buf_a)

    pos1 = (s * ts + lax.broadcasted_iota(jnp.int32, (ts, 1), 0) + 1).astype(F32)
    for g, win in enumerate(POOL_WINDOWS):
        level_buf = buf_a if g % 2 == 0 else buf_b
        cols = slice(g * group, (g + 1) * group)
        inv_cnt = 1.0 / jnp.minimum(pos1, float(win))
        pooled = level_buf[POOL_HALO:n, cols] * inv_cnt - ext[POOL_HALO:n, cols]
        mixed = _dot(pooled.astype(BF16), w_ref[g])
        o_ref[:, cols] = x[:, cols] + mixed * sc_ref[:, cols]

    ext[0:POOL_HALO, :] = ext[ts:n, :]


def _pool_layer(x, gain, w_group, scale, *, ts=512):
    b, s, d = x.shape
    ngroups, group, _ = w_group.shape
    assert s % ts == 0 and ngroups == len(POOL_WINDOWS) and ngroups * group == d
    n = ts + POOL_HALO
    return pl.pallas_call(
        functools.partial(_pool_kernel, ts=ts, group=group),
        grid=(b, s // ts),
        in_specs=[
            pl.BlockSpec((None, ts, d), lambda i, j: (i, j, 0)),
            pl.BlockSpec((1, d), lambda i, j: (0, 0)),
            pl.BlockSpec((ngroups, group, group), lambda i, j: (0, 0, 0)),
            pl.BlockSpec((1, d), lambda i, j: (0, 0)),
        ],
        out_specs=pl.BlockSpec((None, ts, d), lambda i, j: (i, j, 0)),
        out_shape=jax.ShapeDtypeStruct((b, s, d), F32),
        scratch_shapes=[pltpu.VMEM((n, d), F32)] * 3,
        compiler_params=_params("arbitrary", "arbitrary"),
        name="pool_layer",
    )(x, gain.reshape(1, d), w_group, scale.reshape(1, d))


def _sb_proj_kernel(x_ref, g_ref, w_ref, qg_ref, kg_ref, o_ref, h_scr, *, width):
    h_scr[...] = _rms(x_ref[...], g_ref[...]).astype(BF16)
    gains = (qg_ref[...] * (SB_HEAD_DIM ** -0.5 * LOG2E), kg_ref[...], None)
    for tile in range(w_ref.shape[1] // PROJ_TILE):
        cols = slice(tile * PROJ_TILE, (tile + 1) * PROJ_TILE)
        y = _dot(h_scr[...], w_ref[:, cols])
        gain = gains[tile * PROJ_TILE // width]
        if gain is None:
            o_ref[:, cols] = y.astype(o_ref.dtype)
        else:
            for c in range(PROJ_TILE // SB_HEAD_DIM):
                sub = slice(c * SB_HEAD_DIM, (c + 1) * SB_HEAD_DIM)
                o_ref[:, tile * PROJ_TILE + c * SB_HEAD_DIM:tile * PROJ_TILE + (c + 1) * SB_HEAD_DIM] = (
                    _rms(y[:, sub], gain).astype(o_ref.dtype))


def _sb_proj(x2, gain, w_qkv, q_gain, k_gain, *, tm=512):
    t, d = x2.shape
    n = w_qkv.shape[1]
    width = n // 3
    assert t % tm == 0 and width % PROJ_TILE == 0 and PROJ_TILE % SB_HEAD_DIM == 0
    return pl.pallas_call(
        functools.partial(_sb_proj_kernel, width=width),
        grid=(t // tm,),
        in_specs=[
            pl.BlockSpec((tm, d), lambda i: (i, 0)),
            _resident((1, d)),
            _resident((d, n)),
            _resident((1, SB_HEAD_DIM)),
            _resident((1, SB_HEAD_DIM)),
        ],
        out_specs=pl.BlockSpec((tm, n), lambda i: (i, 0)),
        out_shape=jax.ShapeDtypeStruct((t, n), BF16),
        scratch_shapes=[pltpu.VMEM((tm, d), BF16)],
        compiler_params=_params("parallel"),
        name="sb_proj",
    )(x2, gain.reshape(1, d), w_qkv, q_gain.reshape(1, -1), k_gain.reshape(1, -1))


def _sb_attn_kernel(q_ref, k_ref, v_ref, o_ref, *, tq, unroll):
    i = pl.program_id(2)
    tk = tq
    dh = SB_HEAD_DIM
    hp = q_ref.shape[1] // dh
    lanes = [slice(h * dh, (h + 1) * dh) for h in range(hp)]
    q = [q_ref[:, ln] for ln in lanes]
    row = lax.broadcasted_iota(jnp.int32, (tq, tk), 0)
    col = lax.broadcasted_iota(jnp.int32, (tq, tk), 1)
    causal = col < row
    jj = lax.broadcasted_iota(jnp.int32, (tk, tk), 0)
    ss = lax.broadcasted_iota(jnp.int32, (tk, tk), 1)
    suffix = jnp.where(jj > ss, 1.0, 0.0).astype(BF16)

    def scores(h, kb, masked):
        start = pl.multiple_of(kb * tk, tk)
        z = _dot_nt(q[h], k_ref[pl.ds(start, tk), lanes[h]])
        p = jnp.where(z > SOFTPLUS_LINEAR, z, jnp.log(1.0 + jnp.exp2(z)) * LOG2E)
        log_beta = z - p
        if masked:
            p = jnp.where(causal, p, 0.0)
        return p.astype(BF16), log_beta, p[:, :1]

    def blocks(kbs, masks, acc, run):
        acc, run = list(acc), list(run)
        stage1 = [[scores(h, kb, masked) for kb, masked in zip(kbs, masks)] for h in range(hp)]
        afters = [[_dot(p_bf, suffix) for p_bf, _, _ in per_head] for per_head in stage1]
        for n, (kb, masked) in enumerate(zip(kbs, masks)):
            start = pl.multiple_of(kb * tk, tk)
            for h in range(hp):
                _, log_beta, first_col = stage1[h][n]
                after = afters[h][n]
                wts = jnp.exp2(log_beta - (after + run[h]))
                if masked:
                    wts = jnp.where(causal, wts, 0.0)
                acc[h] = acc[h] + _dot(wts.astype(BF16), v_ref[pl.ds(start, tk), lanes[h]])
                run[h] = run[h] + (after[:, :1] + first_col)
        return tuple(acc), tuple(run)

    def live(run):
        return functools.reduce(jnp.minimum, [jnp.min(r) for r in run]) < UNDERFLOW_LOG2

    def head(n_extra):
        def branch():
            acc = tuple(jnp.zeros((tq, dh), F32) for _ in range(hp))
            run = tuple(jnp.zeros((tq, 1), F32) for _ in range(hp))
            kbs = [i] + [i - 1 - u for u in range(n_extra)]
            return blocks(kbs, [True] + [False] * n_extra, acc, run)
        return branch

    extra = jnp.where(i == 0, 0, 1 + (i - 1) % unroll)
    n_groups = jnp.maximum(i - 1, 0) // unroll

    def group(carry):
        t, acc, run, _ = carry
        first = i - 1 - extra - unroll * t
        acc, run = blocks([first - u for u in range(unroll)], [False] * unroll, acc, run)
        return t + 1, acc, run, live(run)

    acc, run = lax.switch(extra, [head(n) for n in range(unroll + 1)])
    _, acc, run, _ = lax.while_loop(lambda c: (c[0] < n_groups) & c[3], group,
                                    (jnp.int32(0), acc, run, live(run)))
    for h in range(hp):
        o_ref[:, lanes[h]] = acc[h].astype(o_ref.dtype)


def _sb_attn(qkv, *, heads, tq=256, unroll=2, hp=2):
    b, s, _ = qkv.shape
    width = hp * SB_HEAD_DIM
    nhb = heads // hp
    assert s % tq == 0 and heads % hp == 0
    return pl.pallas_call(
        functools.partial(_sb_attn_kernel, tq=tq, unroll=unroll),
        grid=(b, nhb, s // tq),
        in_specs=[
            pl.BlockSpec((None, tq, width), lambda bi, h, i: (bi, i, h)),
            pl.BlockSpec((None, s, width), lambda bi, h, i: (bi, 0, nhb + h)),
            pl.BlockSpec((None, s, width), lambda bi, h, i: (bi, 0, 2 * nhb + h)),
        ],
        out_specs=pl.BlockSpec((None, tq, width), lambda bi, h, i: (bi, i, h)),
        out_shape=jax.ShapeDtypeStruct((b, s, heads * SB_HEAD_DIM), BF16),
        compiler_params=_params("parallel", "parallel", "arbitrary"),
        name="sb_attn",
    )(qkv, qkv, qkv)


def _ret_proj_kernel(x_ref, g_ref, w_ref, cos_ref, sin_ref, o_ref, h_scr, *, dk, heads):
    h_scr[...] = _rms(x_ref[...], g_ref[...]).astype(BF16)
    half = dk // 2
    cos = cos_ref[...]
    sin = sin_ref[...]
    for tile in range(w_ref.shape[1] // dk):
        c0 = tile * dk
        y = _dot(h_scr[...], w_ref[:, c0:c0 + dk])
        if tile < 2 * heads:
            if tile >= heads:
                y = y * (dk ** -0.5)
            x1 = y[:, :half]
            x2 = y[:, half:]
            o_ref[:, c0:c0 + half] = (x1 * cos - x2 * sin).astype(o_ref.dtype)
            o_ref[:, c0 + half:c0 + dk] = (x1 * sin + x2 * cos).astype(o_ref.dtype)
        else:
            o_ref[:, c0:c0 + dk] = y.astype(o_ref.dtype)


def _ret_proj(x2, gain, w_in, cos, sin, *, seq, dk, heads, tm=512):
    t, d = x2.shape
    n = w_in.shape[1]
    assert t % tm == 0 and seq % tm == 0 and n % dk == 0 and dk == PROJ_TILE
    per_seq = seq // tm
    return pl.pallas_call(
        functools.partial(_ret_proj_kernel, dk=dk, heads=heads),
        grid=(t // tm,),
        in_specs=[
            pl.BlockSpec((tm, d), lambda i: (i, 0)),
            _resident((1, d)),
            _resident((d, n)),
            pl.BlockSpec((tm, dk // 2), lambda i: (i % per_seq, 0)),
            pl.BlockSpec((tm, dk // 2), lambda i: (i % per_seq, 0)),
        ],
        out_specs=pl.BlockSpec((tm, n), lambda i: (i, 0)),
        out_shape=jax.ShapeDtypeStruct((t, n), BF16),
        scratch_shapes=[pltpu.VMEM((tm, d), BF16)],
        compiler_params=_params("parallel"),
        name="ret_proj",
    )(x2, gain.reshape(1, d), w_in, cos, sin)


def _ret_kernel(lg_ref, q_ref, k_ref, v_ref, g_ref, gn_ref, o_ref, state, *, rows, chunk, heads):
    c = pl.program_id(1)
    dk = q_ref.shape[1] // heads
    dv = v_ref.shape[1] // heads

    @pl.when(c == 0)
    def _():
        state[...] = jnp.zeros_like(state)

    ri = lax.broadcasted_iota(jnp.int32, (chunk, chunk), 0)
    ci = lax.broadcasted_iota(jnp.int32, (chunk, chunk), 1)
    diff = (ri - ci).astype(F32)
    idx = lax.broadcasted_iota(jnp.int32, (chunk, 1), 0).astype(F32)
    lg = [lg_ref[h][:, :1] for h in range(heads)]
    intra = [jnp.where(diff >= 0, jnp.exp(jnp.maximum(diff, 0.0) * l), 0.0) for l in lg]
    q_dec = [jnp.exp((idx + 1.0) * l) for l in lg]
    k_dec = [jnp.exp((chunk - 1.0 - idx) * l) for l in lg]
    chunk_dec = [jnp.exp(chunk * l) for l in lg]

    st = [state[h] for h in range(heads)]
    for n in range(rows // chunk):
        r = slice(n * chunk, (n + 1) * chunk)
        q = [q_ref[r, h * dk:(h + 1) * dk] for h in range(heads)]
        k = [k_ref[r, h * dk:(h + 1) * dk] for h in range(heads)]
        v = [v_ref[r, h * dv:(h + 1) * dv] for h in range(heads)]
        scores = [(_dot_nt(a, b) * m).astype(BF16) for a, b, m in zip(q, k, intra)]
        kd = [(b.astype(F32) * d).astype(BF16) for b, d in zip(k, k_dec)]
        inter = [_dot(a, s.astype(BF16)) for a, s in zip(q, st)]
        kv = [_dot_tn(a, b) for a, b in zip(kd, v)]
        o = [_dot(sc, b) + d * it for sc, b, d, it in zip(scores, v, q_dec, inter)]
        st = [s * d + u for s, d, u in zip(st, chunk_dec, kv)]
        for h in range(heads):
            gate = g_ref[r, h * dv:(h + 1) * dv].astype(F32)
            o_ref[r, h * dv:(h + 1) * dv] = (_rms(o[h], gn_ref[...]) * _silu(gate)).astype(o_ref.dtype)
    for h in range(heads):
        state[h] = st[h]


def _ret_core(proj, lg_rows, gn_gain, *, heads, dk, dv, rows=512):
    b, s, _ = proj.shape
    assert s % rows == 0 and rows % RET_CHUNK == 0 and dv == 2 * dk
    return pl.pallas_call(
        functools.partial(_ret_kernel, rows=rows, chunk=RET_CHUNK, heads=heads),
        grid=(b, s // rows),
        in_specs=[
            _resident((heads, 1, LANES)),
            pl.BlockSpec((None, rows, heads * dk), lambda bi, c: (bi, c, 0)),
            pl.BlockSpec((None, rows, heads * dk), lambda bi, c: (bi, c, 1)),
            pl.BlockSpec((None, rows, heads * dv), lambda bi, c: (bi, c, 1)),
            pl.BlockSpec((None, rows, heads * dv), lambda bi, c: (bi, c, 2)),
            _resident((1, dv)),
        ],
        out_specs=pl.BlockSpec((None, rows, heads * dv), lambda bi, c: (bi, c, 0)),
        out_shape=jax.ShapeDtypeStruct((b, s, heads * dv), BF16),
        scratch_shapes=[pltpu.VMEM((heads, dk, dv), F32)],
        compiler_params=_params("parallel", "arbitrary"),
        name="ret_core",
    )(lg_rows, proj, proj, proj, proj, gn_gain.reshape(1, dv))


def _gdn_proj_kernel(x_ref, g_ref, w_ref, wab_ref, cw_ref, alog_ref, dtb_ref, o_ref, bg_ref,
                     h_scr, halo, *, per_seq, heads, dk):
    i = pl.program_id(0)
    tm = x_ref.shape[0]
    n_conv = cw_ref.shape[1]

    @pl.when(i == 0)
    def _():
        halo[...] = jnp.zeros_like(halo)

    h_scr[...] = _rms(x_ref[...], g_ref[...]).astype(BF16)
    first = (i % per_seq) == 0
    row8 = lax.broadcasted_iota(jnp.int32, (SUBLANES, PROJ_TILE), 0)
    for tile in range(w_ref.shape[1] // PROJ_TILE):
        c0 = tile * PROJ_TILE
        cols = slice(c0, c0 + PROJ_TILE)
        y = _dot(h_scr[...], w_ref[:, cols])
        if c0 >= n_conv:
            o_ref[:, cols] = y.astype(o_ref.dtype)
            continue
        prev = jnp.where(first, 0.0, halo[:, cols])
        halo[:, cols] = y[tm - SUBLANES:tm, :]
        for r0 in range(0, tm, CONV_ROWS):
            rows = slice(r0, r0 + CONV_ROWS)
            yc = y[rows]
            conv = cw_ref[GDN_CONV - 1:GDN_CONV, cols] * yc
            for back in range(1, GDN_CONV):
                shifted = pltpu.roll(yc, back, axis=0)
                top = jnp.where(row8 < back, pltpu.roll(prev, back, axis=0), shifted[:SUBLANES])
                shifted = jnp.concatenate([top, shifted[SUBLANES:]], axis=0)
                conv = conv + cw_ref[GDN_CONV - 1 - back:GDN_CONV - back, cols] * shifted
            prev = yc[CONV_ROWS - SUBLANES:]
            act = _silu(conv)
            if c0 >= 2 * heads * dk:
                o_ref[rows, cols] = act.astype(o_ref.dtype)
                continue
            scale = dk ** -0.5 if c0 < heads * dk else 1.0
            for c in range(PROJ_TILE // dk):
                a = act[:, c * dk:(c + 1) * dk]
                nrm = lax.rsqrt(jnp.sum(a * a, axis=-1, keepdims=True) + EPS) * scale
                o_ref[rows, c0 + c * dk:c0 + (c + 1) * dk] = (a * nrm).astype(o_ref.dtype)

    y = _dot(h_scr[...], wab_ref[...])
    a = y + dtb_ref[...]
    softplus = jnp.maximum(a, 0.0) + jnp.log(1.0 + jnp.exp(-jnp.abs(a)))
    log_decay = -jnp.exp(alog_ref[...]) * softplus
    beta = jax.nn.sigmoid(y)
    lane = lax.broadcasted_iota(jnp.int32, y.shape, 1)
    bg_ref[...] = jnp.where(lane < heads, log_decay, beta)


def _gdn_proj(x2, gain, w_main, w_ab, conv_w, a_log_row, dt_bias_row, *, seq, heads, dk, tm=512):
    t, d = x2.shape
    n = w_main.shape[1]
    n_conv = conv_w.shape[1]
    assert t % tm == 0 and seq % tm == 0 and n % PROJ_TILE == 0 and n_conv % PROJ_TILE == 0
    assert PROJ_TILE % dk == 0 and (heads * dk) % PROJ_TILE == 0
    per_seq = seq // tm
    return pl.pallas_call(
        functools.partial(_gdn_proj_kernel, per_seq=per_seq, heads=heads, dk=dk),
        grid=(t // tm,),
        in_specs=[
            pl.BlockSpec((tm, d), lambda i: (i, 0)),
            _resident((1, d)),
            _resident((d, n)),
            _resident((d, LANES)),
            _resident((GDN_CONV, n_conv)),
            _resident((1, LANES)),
            _resident((1, LANES)),
        ],
        out_specs=[pl.BlockSpec((tm, n), lambda i: (i, 0)), pl.BlockSpec((tm, LANES), lambda i: (i, 0))],
        out_shape=[jax.ShapeDtypeStruct((t, n), BF16), jax.ShapeDtypeStruct((t, LANES), F32)],
        scratch_shapes=[
            pltpu.VMEM((tm, d), BF16),
            pltpu.VMEM((SUBLANES, n_conv), F32),
        ],
        compiler_params=_params("arbitrary"),
        name="gdn_proj",
    )(x2, gain.reshape(1, d), w_main, w_ab, conv_w, a_log_row, dt_bias_row)


def _gdn_kernel(q_ref, k_ref, v_ref, z_ref, bg_ref, gain_ref, o_ref, state, *, rows, chunk, heads, hp):
    hb = pl.program_id(1)
    c = pl.program_id(2)
    C = chunk
    nc = rows // C
    dk = q_ref.shape[1] // hp
    dv = v_ref.shape[1] // hp

    @pl.when(c == 0)
    def _():
        state[...] = jnp.zeros_like(state)

    lane = lax.broadcasted_iota(jnp.int32, (C, LANES), 1)
    ri = lax.broadcasted_iota(jnp.int32, (C, C), 0)
    ci = lax.broadcasted_iota(jnp.int32, (C, C), 1)
    tril = ci <= ri
    strict = ci < ri
    eye = jnp.where(ci == ri, 1.0, 0.0)
    ai = lax.broadcasted_iota(jnp.int32, (2 * C, C), 0)
    am = lax.broadcasted_iota(jnp.int32, (2 * C, C), 1)
    sum_sel = jnp.where(((ai < C) & (am <= ai)) | ((ai >= C) & (am > ai - C)), 1.0, 0.0).astype(BF16)
    sum_sel2 = jnp.concatenate([sum_sel, sum_sel], axis=1)
    bm = lax.broadcasted_iota(jnp.int32, (C, 2 * C), 0)
    bj = lax.broadcasted_iota(jnp.int32, (C, 2 * C), 1)
    col_sel = jnp.where((bm > bj) | (bj >= C), 1.0, 0.0)

    items = [(n, h) for n in range(nc) for h in range(hp)]
    r_of = lambda n: slice(n * C, (n + 1) * C)
    c_of = lambda h, w: slice(h * w, (h + 1) * w)
    bgs = [bg_ref[r_of(n), :] for n in range(nc)]
    pick = lambda n, ln: jnp.sum(jnp.where(lane == ln, bgs[n], 0.0), axis=-1, keepdims=True)
    g_col = [pick(n, hb * hp + h) for n, h in items]
    b_col = [pick(n, hb * hp + h + heads) for n, h in items]

    gmat = [g * col_sel for g in g_col]
    g_hi = [m.astype(BF16) for m in gmat]
    g_split = [jnp.concatenate([hi, (m - hi.astype(F32)).astype(BF16)], axis=0) for m, hi in zip(gmat, g_hi)]
    sums = [_dot(sum_sel2, m) for m in g_split]
    decay = [jnp.where(tril, jnp.exp(s[:C, :C]), 0.0) for s in sums]
    g_cum = [s[:C, C:C + 1] for s in sums]
    exp_g = [jnp.exp(g) for g in g_cum]
    exp_rem = [jnp.exp(s[C:, C:C + 1]) for s in sums]
    exp_last = [e[C - 1:C, :] for e in exp_g]

    q = [q_ref[r_of(n), c_of(h, dk)] for n, h in items]
    k = [k_ref[r_of(n), c_of(h, dk)] for n, h in items]
    kf = [x.astype(F32) for x in k]
    k_beta = [x * b for x, b in zip(kf, b_col)]
    kq = [_dot_nt(jnp.concatenate([kb.astype(BF16), a], axis=0), x) for kb, a, x in zip(k_beta, q, k)]
    n_mat = [jnp.where(strict, m[:C] * d, 0.0) for m, d in zip(kq, decay)]
    attn = [jnp.where(tril, m[C:] * d, 0.0).astype(BF16) for m, d in zip(kq, decay)]
    inv = [eye - m for m in n_mat]
    n_bf = [m.astype(BF16) for m in n_mat]
    power = [_dot(m, m) for m in n_bf]
    for _ in range(4):
        p_bf = [p.astype(BF16) for p in power]
        both = [_dot(jnp.concatenate([x.astype(BF16), p], axis=0), p) for x, p in zip(inv, p_bf)]
        inv = [x + m[:C] for x, m in zip(inv, both)]
        power = [m[C:] for m in both]
    inv = [x + _dot(x.astype(BF16), p.astype(BF16)) for x, p in zip(inv, power)]

    rhs = [jnp.concatenate([v_ref[r_of(n), c_of(h, dv)].astype(F32) * b, kb * eg], axis=1).astype(BF16)
           for (n, h), b, kb, eg in zip(items, b_col, k_beta, exp_g)]
    uw = [_dot(x.astype(BF16), r) for x, r in zip(inv, rhs)]
    kd = [(x * e).astype(BF16) for x, e in zip(kf, exp_rem)]

    st = [state[h] for h in range(hp)]
    for n in range(nc):
        idx = [n * hp + h for h in range(hp)]
        wq = [jnp.concatenate([uw[i][:, dv:].astype(BF16), q[i]], axis=0) for i in idx]
        ws = [_dot(a, s.astype(BF16)) for a, s in zip(wq, st)]
        v_new = [(uw[i][:, :dv] - w[:C]).astype(BF16) for i, w in zip(idx, ws)]
        o = [exp_g[i] * w[C:] + _dot(attn[i], vn) for i, w, vn in zip(idx, ws, v_new)]
        st = [s * exp_last[i] + _dot_tn(kd[i], vn) for i, s, vn in zip(idx, st, v_new)]
        for h in range(hp):
            gate = z_ref[r_of(n), c_of(h, dv)].astype(F32)
            o_ref[r_of(n), c_of(h, dv)] = (_rms(o[h], gain_ref[...]) * _silu(gate)).astype(o_ref.dtype)
    for h in range(hp):
        state[h] = st[h]


def _gdn_core(proj, bg, norm_gain, *, heads, dk, dv, rows=256, hp=8):
    b, s, _ = proj.shape
    assert s % rows == 0 and rows % GDN_CHUNK == 0 and dv == 2 * dk and GDN_CHUNK == 64
    assert heads % hp == 0
    nhb = heads // hp
    return pl.pallas_call(
        functools.partial(_gdn_kernel, rows=rows, chunk=GDN_CHUNK, heads=heads, hp=hp),
        grid=(b, nhb, s // rows),
        in_specs=[
            pl.BlockSpec((None, rows, hp * dk), lambda bi, h, c: (bi, c, h)),
            pl.BlockSpec((None, rows, hp * dk), lambda bi, h, c: (bi, c, nhb + h)),
            pl.BlockSpec((None, rows, hp * dv), lambda bi, h, c: (bi, c, nhb + h)),
            pl.BlockSpec((None, rows, hp * dv), lambda bi, h, c: (bi, c, 2 * nhb + h)),
            pl.BlockSpec((None, rows, LANES), lambda bi, h, c: (bi, c, 0)),
            pl.BlockSpec((1, dv), lambda bi, h, c: (0, 0)),
        ],
        out_specs=pl.BlockSpec((None, rows, hp * dv), lambda bi, h, c: (bi, c, h)),
        out_shape=jax.ShapeDtypeStruct((b, s, heads * dv), BF16),
        scratch_shapes=[pltpu.VMEM((hp, dk, dv), F32)],
        compiler_params=_params("parallel", "parallel", "arbitrary"),
        name="gdn_core",
    )(proj, proj, proj, proj, bg, norm_gain.reshape(1, dv))


def _sb_mixer(x, gain, w_qkv, q_gain, k_gain):
    b, s, d = x.shape
    heads = w_qkv.shape[1] // (3 * SB_HEAD_DIM)
    qkv = _sb_proj(x.reshape(b * s, d), gain, w_qkv.astype(BF16), q_gain, k_gain)
    return _sb_attn(qkv.reshape(b, s, -1), heads=heads)


def _ret_mixer(x, gain, w_in, gn_gain):
    b, s, d = x.shape
    heads = RET_HEADS
    dv = gn_gain.shape[0]
    dk = dv // 2
    half = dk // 2
    inv = ROPE_BASE ** (-jnp.arange(half, dtype=F32) / half)
    ang = jnp.arange(s, dtype=F32)[:, None] * inv[None, :]
    log_gamma = jnp.log1p(-jnp.exp2(-5.0 - jnp.arange(heads, dtype=F32)))
    lg_rows = jnp.broadcast_to(log_gamma[:, None, None], (heads, 1, LANES))
    proj = _ret_proj(x.reshape(b * s, d), gain, w_in.astype(BF16), jnp.cos(ang), jnp.sin(ang),
                     seq=s, dk=dk, heads=heads)
    return _ret_core(proj.reshape(b, s, -1), lg_rows, gn_gain, heads=heads, dk=dk, dv=dv)


def _gdn_mixer(x, gain, w_in, conv_w, a_log, dt_bias, norm_gain):
    b, s, d = x.shape
    heads = a_log.shape[0]
    dk = GDN_K_DIM
    dv = norm_gain.shape[0]
    n_main = 2 * heads * dk + 2 * heads * dv
    w_ab = jnp.zeros((d, LANES), BF16).at[:, :2 * heads].set(w_in[:, n_main:].astype(BF16))
    pad = lambda p: jnp.zeros((1, LANES), F32).at[0, :heads].set(p)
    proj, bg = _gdn_proj(x.reshape(b * s, d), gain, w_in[:, :n_main].astype(BF16), w_ab, conv_w,
                         pad(a_log), pad(dt_bias), seq=s, heads=heads, dk=dk)
    return _gdn_core(proj.reshape(b, s, -1), bg.reshape(b, s, LANES), norm_gain, heads=heads, dk=dk, dv=dv)


def _ffn_layer(x, gain, w_in, w_out, o=None, w_o=None):
    b, s, d = x.shape
    if o is not None:
        o, w_o = o.reshape(b * s, -1), w_o.astype(BF16)
    return _ffn(x.reshape(b * s, d), gain, w_in.astype(BF16), w_out.astype(BF16), o, w_o).reshape(b, s, d)


def kernel(x, norm_mix, norm_ffn, ffn_w_in, ffn_w_out, pool_w, pool_scale, sb_w_qkv, sb_q_gain, sb_k_gain, sb_w_o, ret_w_in, ret_gn_gain, ret_w_o, gdn_w_in, gdn_conv_w, gdn_a_log, gdn_dt_bias, gdn_norm_gain, gdn_w_o):
    depth = norm_mix.shape[0]
    for layer in range(depth):
        mixer, occ = layer % 4, layer // 4
        o, w_o = None, None
        if mixer == 0:
            x = _pool_layer(x, norm_mix[layer], pool_w[occ].astype(BF16), pool_scale[occ])
        elif mixer == 1:
            o, w_o = _sb_mixer(x, norm_mix[layer], sb_w_qkv[occ], sb_q_gain[occ], sb_k_gain[occ]), sb_w_o[occ]
        elif mixer == 2:
            o, w_o = _ret_mixer(x, norm_mix[layer], ret_w_in[occ], ret_gn_gain[occ]), ret_w_o[occ]
        else:
            o, w_o = _gdn_mixer(x, norm_mix[layer], gdn_w_in[occ], gdn_conv_w[occ], gdn_a_log[occ],
                                gdn_dt_bias[occ], gdn_norm_gain[occ]), gdn_w_o[occ]
        x = _ffn_layer(x, norm_ffn[layer], ffn_w_in[layer], ffn_w_out[layer], o, w_o)
    return x
```

```python
import functools

import jax
import jax.numpy as jnp
from jax import lax
from jax.experimental import pallas as pl
from jax.experimental.pallas import tpu as pltpu

F32 = jnp.float32
BF16 = jnp.bfloat16

EPS = 1e-6
LOG2E = 1.4426950408889634
SOFTPLUS_LINEAR = 64.0
UNDERFLOW_LOG2 = 160.0
ROPE_BASE = 10000.0
LANES = 128
SUBLANES = 8
VMEM_LIMIT = 56 * 1024 * 1024
PROJ_TILE = 256

POOL_WINDOWS = (2, 4, 8, 16)
POOL_HALO = 32
SB_HEAD_DIM = 128
RET_HEADS = 4
RET_CHUNK = 128
GDN_K_DIM = 128
GDN_CONV = 4
GDN_CHUNK = 64
CONV_ROWS = 64


def _params(*sem):
    return pltpu.CompilerParams(dimension_semantics=sem, vmem_limit_bytes=VMEM_LIMIT)


def _resident(shape):
    return pl.BlockSpec(shape, lambda *_: (0,) * len(shape), pipeline_mode=pl.Buffered(1))


def _rms(xf, gain_row):
    ms = jnp.mean(xf * xf, axis=-1, keepdims=True)
    return xf * lax.rsqrt(ms + EPS) * gain_row


def _silu(x):
    h = 0.5 * x
    return h + h * jnp.tanh(h)


def _dot(a, b):
    return jnp.dot(a, b, preferred_element_type=F32)


def _dot_nt(a, b):
    return lax.dot_general(a, b, (((1,), (1,)), ((), ())), preferred_element_type=F32)


def _dot_tn(a, b):
    return lax.dot_general(a, b, (((0,), (0,)), ((), ())), preferred_element_type=F32)


def _ffn_kernel(*refs, mixer_proj):
    if mixer_proj:
        x_ref, a_ref, wo_ref, g_ref, wi_ref, wd_ref, o_ref, h_scr, act_scr, x1_scr = refs
        x1_scr[...] = x_ref[...] + _dot(a_ref[...], wo_ref[...])
        x1_ref = x1_scr
    else:
        x_ref, g_ref, wi_ref, wd_ref, o_ref, h_scr, act_scr = refs
        x1_ref = x_ref
    h_scr[...] = _rms(x1_ref[...], g_ref[...]).astype(BF16)
    hidden = wd_ref.shape[0]
    for f0 in range(0, hidden, PROJ_TILE):
        gate = _dot(h_scr[...], wi_ref[:, f0:f0 + PROJ_TILE])
        up = _dot(h_scr[...], wi_ref[:, hidden + f0:hidden + f0 + PROJ_TILE])
        act_scr[:, f0:f0 + PROJ_TILE] = (_silu(gate) * up).astype(BF16)
    for c0 in range(0, o_ref.shape[1], PROJ_TILE):
        cols = slice(c0, c0 + PROJ_TILE)
        o_ref[:, cols] = x1_ref[:, cols] + _dot(act_scr[...], wd_ref[:, cols])


def _ffn(x2, gain, w_in, w_out, a2=None, w_o=None, *, tm=512):
    t, d = x2.shape
    hidden = w_out.shape[0]
    assert t % tm == 0 and hidden % PROJ_TILE == 0 and d % PROJ_TILE == 0
    row = lambda width: pl.BlockSpec((tm, width), lambda i: (i, 0))
    mixer_proj = a2 is not None
    in_specs = [row(d)]
    args = [x2]
    scratch = [pltpu.VMEM((tm, d), BF16), pltpu.VMEM((tm, hidden), BF16)]
    if mixer_proj:
        in_specs += [row(a2.shape[1]), _resident(w_o.shape)]
        args += [a2, w_o]
        scratch.append(pltpu.VMEM((tm, d), F32))
    in_specs += [_resident((1, d)), _resident(w_in.shape), _resident(w_out.shape)]
    args += [gain.reshape(1, d), w_in, w_out]
    return pl.pallas_call(
        functools.partial(_ffn_kernel, mixer_proj=mixer_proj),
        grid=(t // tm,),
        in_specs=in_specs,
        out_specs=row(d),
        out_shape=jax.ShapeDtypeStruct((t, d), F32),
        scratch_shapes=scratch,
        compiler_params=_params("parallel"),
        name="ffn",
    )(*args)


def _pool_kernel(x_ref, g_ref, w_ref, sc_ref, o_ref, ext, buf_a, buf_b, *, ts, group):
    s = pl.program_id(1)
    n = ts + POOL_HALO
    d = x_ref.shape[-1]

    @pl.when(s == 0)
    def _():
        ext[0:POOL_HALO, :] = jnp.zeros((POOL_HALO, d), F32)

    x = x_ref[...]
    ext[POOL_HALO:n, :] = _rms(x, g_ref[...])

    src, dst = ext, buf_a
    for level in range(1, len(POOL_WINDOWS) + 1):
        shift = 1 << (level - 1)
        lo = SUBLANES * level
        c0 = (level - 1) * group
        dst[lo:n, c0:d] = src[lo:n, c0:d] + src[lo - shift:n - shift, c0:d]
        src, dst = dst, (buf_b if dst is buf_a else buf_a)

    pos1 = (s * ts + lax.broadcasted_iota(jnp.int32, (ts, 1), 0) + 1).astype(F32)
    for g, win in enumerate(POOL_WINDOWS):
        level_buf = buf_a if g % 2 == 0 else buf_b
        cols = slice(g * group, (g + 1) * group)
        inv_cnt = 1.0 / jnp.minimum(pos1, float(win))
        pooled = level_buf[POOL_HALO:n, cols] * inv_cnt - ext[POOL_HALO:n, cols]
        mixed = _dot(pooled.astype(BF16), w_ref[g])
        o_ref[:, cols] = x[:, cols] + mixed * sc_ref[:, cols]

    ext[0:POOL_HALO, :] = ext[ts:n, :]


def _pool_layer(x, gain, w_group, scale, *, ts=512):
    b, s, d = x.shape
    ngroups, group, _ = w_group.shape
    assert s % ts == 0 and ngroups == len(POOL_WINDOWS) and ngroups * group == d
    n = ts + POOL_HALO
    return pl.pallas_call(
        functools.partial(_pool_kernel, ts=ts, group=group),
        grid=(b, s // ts),
        in_specs=[
            pl.BlockSpec((None, ts, d), lambda i, j: (i, j, 0)),
            pl.BlockSpec((1, d), lambda i, j: (0, 0)),
            pl.BlockSpec((ngroups, group, group), lambda i, j: (0, 0, 0)),
            pl.BlockSpec((1, d), lambda i, j: (0, 0)),
        ],
        out_specs=pl.BlockSpec((None, ts, d), lambda i, j: (i, j, 0)),
        out_shape=jax.ShapeDtypeStruct((b, s, d), F32),
        scratch_shapes=[pltpu.VMEM((n, d), F32)] * 3,
        compiler_params=_params("arbitrary", "arbitrary"),
        name="pool_layer",
    )(x, gain.reshape(1, d), w_group, scale.reshape(1, d))


def _sb_proj_kernel(x_ref, g_ref, w_ref, qg_ref, kg_ref, o_ref, h_scr, *, width):
    h_scr[...] = _rms(x_ref[...], g_ref[...]).astype(BF16)
    gains = (qg_ref[...] * (SB_HEAD_DIM ** -0.5 * LOG2E), kg_ref[...], None)
    for tile in range(w_ref.shape[1] // PROJ_TILE):
        cols = slice(tile * PROJ_TILE, (tile + 1) * PROJ_TILE)
        y = _dot(h_scr[...], w_ref[:, cols])
        gain = gains[tile * PROJ_TILE // width]
        if gain is None:
            o_ref[:, cols] = y.astype(o_ref.dtype)
        else:
            for c in range(PROJ_TILE // SB_HEAD_DIM):
                sub = slice(c * SB_HEAD_DIM, (c + 1) * SB_HEAD_DIM)
                o_ref[:, tile * PROJ_TILE + c * SB_HEAD_DIM:tile * PROJ_TILE + (c + 1) * SB_HEAD_DIM] = (
                    _rms(y[:, sub], gain).astype(o_ref.dtype))


def _sb_proj(x2, gain, w_qkv, q_gain, k_gain, *, tm=512):
    t, d = x2.shape
    n = w_qkv.shape[1]
    width = n // 3
    assert t % tm == 0 and width % PROJ_TILE == 0 and PROJ_TILE % SB_HEAD_DIM == 0
    return pl.pallas_call(
        functools.partial(_sb_proj_kernel, width=width),
        grid=(t // tm,),
        in_specs=[
            pl.BlockSpec((tm, d), lambda i: (i, 0)),
            _resident((1, d)),
            _resident((d, n)),
            _resident((1, SB_HEAD_DIM)),
            _resident((1, SB_HEAD_DIM)),
        ],
        out_specs=pl.BlockSpec((tm, n), lambda i: (i, 0)),
        out_shape=jax.ShapeDtypeStruct((t, n), BF16),
        scratch_shapes=[pltpu.VMEM((tm, d), BF16)],
        compiler_params=_params("parallel"),
        name="sb_proj",
    )(x2, gain.reshape(1, d), w_qkv, q_gain.reshape(1, -1), k_gain.reshape(1, -1))


def _sb_attn_kernel(q_ref, k_ref, v_ref, o_ref, *, tq, unroll):
    i = pl.program_id(2)
    tk = tq
    dh = SB_HEAD_DIM
    hp = q_ref.shape[1] // dh
    lanes = [slice(h * dh, (h + 1) * dh) for h in range(hp)]
    q = [q_ref[:, ln] for ln in lanes]
    row = lax.broadcasted_iota(jnp.int32, (tq, tk), 0)
    col = lax.broadcasted_iota(jnp.int32, (tq, tk), 1)
    causal = col < row
    jj = lax.broadcasted_iota(jnp.int32, (tk, tk), 0)
    ss = lax.broadcasted_iota(jnp.int32, (tk, tk), 1)
    suffix = jnp.where(jj > ss, 1.0, 0.0).astype(BF16)

    def scores(h, kb, keep):
        start = pl.multiple_of(kb * tk, tk)
        z = _dot_nt(q[h], k_ref[pl.ds(start, tk), lanes[h]])
        p = jnp.where(z > SOFTPLUS_LINEAR, z, jnp.log(1.0 + jnp.exp2(z)) * LOG2E)
        log_beta = z - p
        if keep is not None:
            p = jnp.where(keep, p, 0.0)
        return p.astype(BF16), log_beta, p[:, :1]

    def blocks(kbs, keeps, acc, run):
        acc, run = list(acc), list(run)
        stage1 = [[scores(h, kb, keep) for kb, keep in zip(kbs, keeps)] for h in range(hp)]
        afters = [[_dot(p_bf, suffix) for p_bf, _, _ in per_head] for per_head in stage1]
        for n, (kb, keep) in enumerate(zip(kbs, keeps)):
            start = pl.multiple_of(kb * tk, tk)
            for h in range(hp):
                _, log_beta, first_col = stage1[h][n]
                after = afters[h][n]
                wts = jnp.exp2(log_beta - (after + run[h]))
                if keep is not None:
                    wts = jnp.where(keep, wts, 0.0)
                acc[h] = acc[h] + _dot(wts.astype(BF16), v_ref[pl.ds(start, tk), lanes[h]])
                run[h] = run[h] + (after[:, :1] + first_col)
        return tuple(acc), tuple(run)

    def live(run):
        return functools.reduce(jnp.minimum, [jnp.min(r) for r in run]) < UNDERFLOW_LOG2

    def head(n_extra):
        def branch():
            acc = tuple(jnp.zeros((tq, dh), F32) for _ in range(hp))
            run = tuple(jnp.zeros((tq, 1), F32) for _ in range(hp))
            kbs = [i] + [i - 1 - u for u in range(n_extra)]
            return blocks(kbs, [causal] + [None] * n_extra, acc, run)
        return branch

    n_groups = (jnp.maximum(i - 1, 0) + unroll - 1) // unroll

    def group(carry):
        t, acc, run, _ = carry
        kbs = [i - 2 - unroll * t - u for u in range(unroll)]
        acc, run = blocks([jnp.maximum(kb, 0) for kb in kbs], [kb >= 0 for kb in kbs], acc, run)
        return t + 1, acc, run, live(run)

    acc, run = lax.switch(jnp.minimum(i, 1), [head(0), head(1)])
    _, acc, run, _ = lax.while_loop(lambda c: (c[0] < n_groups) & c[3], group,
                                    (jnp.int32(0), acc, run, live(run)))
    for h in range(hp):
        o_ref[:, lanes[h]] = acc[h].astype(o_ref.dtype)


def _sb_attn(qkv, *, heads, tq=256, unroll=2, hp=4):
    b, s, _ = qkv.shape
    width = hp * SB_HEAD_DIM
    nhb = heads // hp
    assert s % tq == 0 and heads % hp == 0
    return pl.pallas_call(
        functools.partial(_sb_attn_kernel, tq=tq, unroll=unroll),
        grid=(b, nhb, s // tq),
        in_specs=[
            pl.BlockSpec((None, tq, width), lambda bi, h, i: (bi, i, h)),
            pl.BlockSpec((None, s, width), lambda bi, h, i: (bi, 0, nhb + h)),
            pl.BlockSpec((None, s, width), lambda bi, h, i: (bi, 0, 2 * nhb + h)),
        ],
        out_specs=pl.BlockSpec((None, tq, width), lambda bi, h, i: (bi, i, h)),
        out_shape=jax.ShapeDtypeStruct((b, s, heads * SB_HEAD_DIM), BF16),
        compiler_params=_params("parallel", "parallel", "arbitrary"),
        name="sb_attn",
    )(qkv, qkv, qkv)


def _ret_proj_kernel(x_ref, g_ref, w_ref, cos_ref, sin_ref, o_ref, h_scr, *, dk, heads):
    h_scr[...] = _rms(x_ref[...], g_ref[...]).astype(BF16)
    half = dk // 2
    cos = cos_ref[...]
    sin = sin_ref[...]
    for tile in range(w_ref.shape[1] // dk):
        c0 = tile * dk
        y = _dot(h_scr[...], w_ref[:, c0:c0 + dk])
        if tile < 2 * heads:
            if tile >= heads:
                y = y * (dk ** -0.5)
            x1 = y[:, :half]
            x2 = y[:, half:]
            o_ref[:, c0:c0 + half] = (x1 * cos - x2 * sin).astype(o_ref.dtype)
            o_ref[:, c0 + half:c0 + dk] = (x1 * sin + x2 * cos).astype(o_ref.dtype)
        else:
            o_ref[:, c0:c0 + dk] = y.astype(o_ref.dtype)


def _ret_proj(x2, gain, w_in, cos, sin, *, seq, dk, heads, tm=512):
    t, d = x2.shape
    n = w_in.shape[1]
    assert t % tm == 0 and seq % tm == 0 and n % dk == 0 and dk == PROJ_TILE
    per_seq = seq // tm
    return pl.pallas_call(
        functools.partial(_ret_proj_kernel, dk=dk, heads=heads),
        grid=(t // tm,),
        in_specs=[
            pl.BlockSpec((tm, d), lambda i: (i, 0)),
            _resident((1, d)),
            _resident((d, n)),
            pl.BlockSpec((tm, dk // 2), lambda i: (i % per_seq, 0)),
            pl.BlockSpec((tm, dk // 2), lambda i: (i % per_seq, 0)),
        ],
        out_specs=pl.BlockSpec((tm, n), lambda i: (i, 0)),
        out_shape=jax.ShapeDtypeStruct((t, n), BF16),
        scratch_shapes=[pltpu.VMEM((tm, d), BF16)],
        compiler_params=_params("parallel"),
        name="ret_proj",
    )(x2, gain.reshape(1, d), w_in, cos, sin)


def _ret_kernel(lg_ref, q_ref, k_ref, v_ref, g_ref, gn_ref, o_ref, state, *, rows, chunk, heads):
    c = pl.program_id(1)
    dk = q_ref.shape[1] // heads
    dv = v_ref.shape[1] // heads

    @pl.when(c == 0)
    def _():
        state[...] = jnp.zeros_like(state)

    ri = lax.broadcasted_iota(jnp.int32, (chunk, chunk), 0)
    ci = lax.broadcasted_iota(jnp.int32, (chunk, chunk), 1)
    diff = (ri - ci).astype(F32)
    idx = lax.broadcasted_iota(jnp.int32, (chunk, 1), 0).astype(F32)
    lg = [lg_ref[h][:, :1] for h in range(heads)]
    intra = [jnp.where(diff >= 0, jnp.exp(jnp.maximum(diff, 0.0) * l), 0.0) for l in lg]
    q_dec = [jnp.exp((idx + 1.0) * l) for l in lg]
    k_dec = [jnp.exp((chunk - 1.0 - idx) * l) for l in lg]
    chunk_dec = [jnp.exp(chunk * l) for l in lg]

    st = [state[h] for h in range(heads)]
    for n in range(rows // chunk):
        r = slice(n * chunk, (n + 1) * chunk)
        q = [q_ref[r, h * dk:(h + 1) * dk] for h in range(heads)]
        k = [k_ref[r, h * dk:(h + 1) * dk] for h in range(heads)]
        v = [v_ref[r, h * dv:(h + 1) * dv] for h in range(heads)]
        scores = [(_dot_nt(a, b) * m).astype(BF16) for a, b, m in zip(q, k, intra)]
        kd = [(b.astype(F32) * d).astype(BF16) for b, d in zip(k, k_dec)]
        inter = [_dot(a, s.astype(BF16)) for a, s in zip(q, st)]
        kv = [_dot_tn(a, b) for a, b in zip(kd, v)]
        o = [_dot(sc, b) + d * it for sc, b, d, it in zip(scores, v, q_dec, inter)]
        st = [s * d + u for s, d, u in zip(st, chunk_dec, kv)]
        for h in range(heads):
            gate = g_ref[r, h * dv:(h + 1) * dv].astype(F32)
            o_ref[r, h * dv:(h + 1) * dv] = (_rms(o[h], gn_ref[...]) * _silu(gate)).astype(o_ref.dtype)
    for h in range(heads):
        state[h] = st[h]


def _ret_core(proj, lg_rows, gn_gain, *, heads, dk, dv, rows=512):
    b, s, _ = proj.shape
    assert s % rows == 0 and rows % RET_CHUNK == 0 and dv == 2 * dk
    return pl.pallas_call(
        functools.partial(_ret_kernel, rows=rows, chunk=RET_CHUNK, heads=heads),
        grid=(b, s // rows),
        in_specs=[
            _resident((heads, 1, LANES)),
            pl.BlockSpec((None, rows, heads * dk), lambda bi, c: (bi, c, 0)),
            pl.BlockSpec((None, rows, heads * dk), lambda bi, c: (bi, c, 1)),
            pl.BlockSpec((None, rows, heads * dv), lambda bi, c: (bi, c, 1)),
            pl.BlockSpec((None, rows, heads * dv), lambda bi, c: (bi, c, 2)),
            _resident((1, dv)),
        ],
        out_specs=pl.BlockSpec((None, rows, heads * dv), lambda bi, c: (bi, c, 0)),
        out_shape=jax.ShapeDtypeStruct((b, s, heads * dv), BF16),
        scratch_shapes=[pltpu.VMEM((heads, dk, dv), F32)],
        compiler_params=_params("parallel", "arbitrary"),
        name="ret_core",
    )(lg_rows, proj, proj, proj, proj, gn_gain.reshape(1, dv))


def _gdn_proj_kernel(x_ref, g_ref, w_ref, wab_ref, cw_ref, alog_ref, dtb_ref, o_ref, bg_ref,
                     h_scr, halo, *, per_seq, heads, dk):
    i = pl.program_id(0)
    tm = x_ref.shape[0]
    n_conv = cw_ref.shape[1]

    @pl.when(i == 0)
    def _():
        halo[...] = jnp.zeros_like(halo)

    h_scr[...] = _rms(x_ref[...], g_ref[...]).astype(BF16)
    first = (i % per_seq) == 0
    row8 = lax.broadcasted_iota(jnp.int32, (SUBLANES, PROJ_TILE), 0)
    conv_tiles = list(range(n_conv // PROJ_TILE))
    plain_tiles = list(range(n_conv // PROJ_TILE, w_ref.shape[1] // PROJ_TILE))
    per_plain = -(-len(conv_tiles) // max(len(plain_tiles), 1))
    order = []
    for n, tile in enumerate(conv_tiles):
        order.append(tile)
        if (n + 1) % per_plain == 0 and plain_tiles:
            order.append(plain_tiles.pop(0))
    order += plain_tiles
    for tile in order:
        c0 = tile * PROJ_TILE
        cols = slice(c0, c0 + PROJ_TILE)
        y = _dot(h_scr[...], w_ref[:, cols])
        if c0 >= n_conv:
            o_ref[:, cols] = y.astype(o_ref.dtype)
            continue
        prev = jnp.where(first, 0.0, halo[:, cols])
        halo[:, cols] = y[tm - SUBLANES:tm, :]
        for r0 in range(0, tm, CONV_ROWS):
            rows = slice(r0, r0 + CONV_ROWS)
            yc = y[rows]
            conv = cw_ref[GDN_CONV - 1:GDN_CONV, cols] * yc
            for back in range(1, GDN_CONV):
                shifted = pltpu.roll(yc, back, axis=0)
                top = jnp.where(row8 < back, pltpu.roll(prev, back, axis=0), shifted[:SUBLANES])
                shifted = jnp.concatenate([top, shifted[SUBLANES:]], axis=0)
                conv = conv + cw_ref[GDN_CONV - 1 - back:GDN_CONV - back, cols] * shifted
            prev = yc[CONV_ROWS - SUBLANES:]
            act = _silu(conv)
            if c0 >= 2 * heads * dk:
                o_ref[rows, cols] = act.astype(o_ref.dtype)
                continue
            scale = dk ** -0.5 if c0 < heads * dk else 1.0
            for c in range(PROJ_TILE // dk):
                a = act[:, c * dk:(c + 1) * dk]
                nrm = lax.rsqrt(jnp.sum(a * a, axis=-1, keepdims=True) + EPS) * scale
                o_ref[rows, c0 + c * dk:c0 + (c + 1) * dk] = (a * nrm).astype(o_ref.dtype)

    y = _dot(h_scr[...], wab_ref[...])
    a = y + dtb_ref[...]
    softplus = jnp.maximum(a, 0.0) + jnp.log(1.0 + jnp.exp(-jnp.abs(a)))
    log_decay = -jnp.exp(alog_ref[...]) * softplus
    beta = jax.nn.sigmoid(y)
    lane = lax.broadcasted_iota(jnp.int32, y.shape, 1)
    bg_ref[...] = jnp.where(lane < heads, log_decay, beta)


def _gdn_proj(x2, gain, w_main, w_ab, conv_w, a_log_row, dt_bias_row, *, seq, heads, dk, tm=512):
    t, d = x2.shape
    n = w_main.shape[1]
    n_conv = conv_w.shape[1]
    assert t % tm == 0 and seq % tm == 0 and n % PROJ_TILE == 0 and n_conv % PROJ_TILE == 0
    assert PROJ_TILE % dk == 0 and (heads * dk) % PROJ_TILE == 0
    per_seq = seq // tm
    return pl.pallas_call(
        functools.partial(_gdn_proj_kernel, per_seq=per_seq, heads=heads, dk=dk),
        grid=(t // tm,),
        in_specs=[
            pl.BlockSpec((tm, d), lambda i: (i, 0)),
            _resident((1, d)),
            _resident((d, n)),
            _resident((d, LANES)),
            _resident((GDN_CONV, n_conv)),
            _resident((1, LANES)),
            _resident((1, LANES)),
        ],
        out_specs=[pl.BlockSpec((tm, n), lambda i: (i, 0)), pl.BlockSpec((tm, LANES), lambda i: (i, 0))],
        out_shape=[jax.ShapeDtypeStruct((t, n), BF16), jax.ShapeDtypeStruct((t, LANES), F32)],
        scratch_shapes=[
            pltpu.VMEM((tm, d), BF16),
            pltpu.VMEM((SUBLANES, n_conv), F32),
        ],
        compiler_params=_params("arbitrary"),
        name="gdn_proj",
    )(x2, gain.reshape(1, d), w_main, w_ab, conv_w, a_log_row, dt_bias_row)


def _gdn_kernel(q_ref, k_ref, v_ref, z_ref, bg_ref, gain_ref, o_ref, state, *, rows, chunk, heads, hp):
    hb = pl.program_id(1)
    c = pl.program_id(2)
    C = chunk
    nc = rows // C
    dk = q_ref.shape[1] // hp
    dv = v_ref.shape[1] // hp

    @pl.when(c == 0)
    def _():
        state[...] = jnp.zeros_like(state)

    lane = lax.broadcasted_iota(jnp.int32, (C, LANES), 1)
    ri = lax.broadcasted_iota(jnp.int32, (C, C), 0)
    ci = lax.broadcasted_iota(jnp.int32, (C, C), 1)
    tril = ci <= ri
    strict = ci < ri
    eye = jnp.where(ci == ri, 1.0, 0.0)
    ai = lax.broadcasted_iota(jnp.int32, (2 * C, C), 0)
    am = lax.broadcasted_iota(jnp.int32, (2 * C, C), 1)
    sum_sel = jnp.where(((ai < C) & (am <= ai)) | ((ai >= C) & (am > ai - C)), 1.0, 0.0).astype(BF16)
    sum_sel2 = jnp.concatenate([sum_sel, sum_sel], axis=1)
    bm = lax.broadcasted_iota(jnp.int32, (C, 2 * C), 0)
    bj = lax.broadcasted_iota(jnp.int32, (C, 2 * C), 1)
    col_sel = jnp.where((bm > bj) | (bj >= C), 1.0, 0.0)

    items = [(n, h) for n in range(nc) for h in range(hp)]
    r_of = lambda n: slice(n * C, (n + 1) * C)
    c_of = lambda h, w: slice(h * w, (h + 1) * w)
    bgs = [bg_ref[r_of(n), :] for n in range(nc)]
    pick = lambda n, ln: jnp.sum(jnp.where(lane == ln, bgs[n], 0.0), axis=-1, keepdims=True)
    g_col = [pick(n, hb * hp + h) for n, h in items]
    b_col = [pick(n, hb * hp + h + heads) for n, h in items]

    gmat = [g * col_sel for g in g_col]
    g_hi = [m.astype(BF16) for m in gmat]
    g_split = [jnp.concatenate([hi, (m - hi.astype(F32)).astype(BF16)], axis=0) for m, hi in zip(gmat, g_hi)]
    sums = [_dot(sum_sel2, m) for m in g_split]
    decay = [jnp.where(tril, jnp.exp(s[:C, :C]), 0.0) for s in sums]
    g_cum = [s[:C, C:C + 1] for s in sums]
    exp_g = [jnp.exp(g) for g in g_cum]
    exp_rem = [jnp.exp(s[C:, C:C + 1]) for s in sums]
    exp_last = [e[C - 1:C, :] for e in exp_g]

    q = [q_ref[r_of(n), c_of(h, dk)] for n, h in items]
    k = [k_ref[r_of(n), c_of(h, dk)] for n, h in items]
    kf = [x.astype(F32) for x in k]
    k_beta = [x * b for x, b in zip(kf, b_col)]
    kq = [_dot_nt(jnp.concatenate([kb.astype(BF16), a], axis=0), x) for kb, a, x in zip(k_beta, q, k)]
    n_mat = [jnp.where(strict, m[:C] * d, 0.0) for m, d in zip(kq, decay)]
    attn = [jnp.where(tril, m[C:] * d, 0.0).astype(BF16) for m, d in zip(kq, decay)]
    inv = [eye - m for m in n_mat]
    n_bf = [m.astype(BF16) for m in n_mat]
    power = [_dot(m, m) for m in n_bf]
    for _ in range(4):
        p_bf = [p.astype(BF16) for p in power]
        both = [_dot(jnp.concatenate([x.astype(BF16), p], axis=0), p) for x, p in zip(inv, p_bf)]
        inv = [x + m[:C] for x, m in zip(inv, both)]
        power = [m[C:] for m in both]
    inv = [x + _dot(x.astype(BF16), p.astype(BF16)) for x, p in zip(inv, power)]

    rhs = [jnp.concatenate([v_ref[r_of(n), c_of(h, dv)].astype(F32) * b, kb * eg], axis=1).astype(BF16)
           for (n, h), b, kb, eg in zip(items, b_col, k_beta, exp_g)]
    uw = [_dot(x.astype(BF16), r) for x, r in zip(inv, rhs)]
    kd = [(x * e).astype(BF16) for x, e in zip(kf, exp_rem)]

    st = [state[h] for h in range(hp)]
    for n in range(nc):
        idx = [n * hp + h for h in range(hp)]
        wq = [jnp.concatenate([uw[i][:, dv:].astype(BF16), q[i]], axis=0) for i in idx]
        ws = [_dot(a, s.astype(BF16)) for a, s in zip(wq, st)]
        v_new = [(uw[i][:, :dv] - w[:C]).astype(BF16) for i, w in zip(idx, ws)]
        o = [exp_g[i] * w[C:] + _dot(attn[i], vn) for i, w, vn in zip(idx, ws, v_new)]
        st = [s * exp_last[i] + _dot_tn(kd[i], vn) for i, s, vn in zip(idx, st, v_new)]
        for h in range(hp):
            gate = z_ref[r_of(n), c_of(h, dv)].astype(F32)
            o_ref[r_of(n), c_of(h, dv)] = (_rms(o[h], gain_ref[...]) * _silu(gate)).astype(o_ref.dtype)
    for h in range(hp):
        state[h] = st[h]


def _gdn_core(proj, bg, norm_gain, *, heads, dk, dv, rows=256, hp=8):
    b, s, _ = proj.shape
    assert s % rows == 0 and rows % GDN_CHUNK == 0 and dv == 2 * dk and GDN_CHUNK == 64
    assert heads % hp == 0
    nhb = heads // hp
    return pl.pallas_call(
        functools.partial(_gdn_kernel, rows=rows, chunk=GDN_CHUNK, heads=heads, hp=hp),
        grid=(b, nhb, s // rows),
        in_specs=[
            pl.BlockSpec((None, rows, hp * dk), lambda bi, h, c: (bi, c, h)),
            pl.BlockSpec((None, rows, hp * dk), lambda bi, h, c: (bi, c, nhb + h)),
            pl.BlockSpec((None, rows, hp * dv), lambda bi, h, c: (bi, c, nhb + h)),
            pl.BlockSpec((None, rows, hp * dv), lambda bi, h, c: (bi, c, 2 * nhb + h)),
            pl.BlockSpec((None, rows, LANES), lambda bi, h, c: (bi, c, 0)),
            pl.BlockSpec((1, dv), lambda bi, h, c: (0, 0)),
        ],
        out_specs=pl.BlockSpec((None, rows, hp * dv), lambda bi, h, c: (bi, c, h)),
        out_shape=jax.ShapeDtypeStruct((b, s, heads * dv), BF16),
        scratch_shapes=[pltpu.VMEM((hp, dk, dv), F32)],
        compiler_params=_params("parallel", "parallel", "arbitrary"),
        name="gdn_core",
    )(proj, proj, proj, proj, bg, norm_gain.reshape(1, dv))


def _sb_mixer(x, gain, w_qkv, q_gain, k_gain):
    b, s, d = x.shape
    heads = w_qkv.shape[1] // (3 * SB_HEAD_DIM)
    qkv = _sb_proj(x.reshape(b * s, d), gain, w_qkv.astype(BF16), q_gain, k_gain)
    return _sb_attn(qkv.reshape(b, s, -1), heads=heads)


def _ret_mixer(x, gain, w_in, gn_gain):
    b, s, d = x.shape
    heads = RET_HEADS
    dv = gn_gain.shape[0]
    dk = dv // 2
    half = dk // 2
    inv = ROPE_BASE ** (-jnp.arange(half, dtype=F32) / half)
    ang = jnp.arange(s, dtype=F32)[:, None] * inv[None, :]
    log_gamma = jnp.log1p(-jnp.exp2(-5.0 - jnp.arange(heads, dtype=F32)))
    lg_rows = jnp.broadcast_to(log_gamma[:, None, None], (heads, 1, LANES))
    proj = _ret_proj(x.reshape(b * s, d), gain, w_in.astype(BF16), jnp.cos(ang), jnp.sin(ang),
                     seq=s, dk=dk, heads=heads)
    return _ret_core(proj.reshape(b, s, -1), lg_rows, gn_gain, heads=heads, dk=dk, dv=dv)


def _gdn_mixer(x, gain, w_in, conv_w, a_log, dt_bias, norm_gain):
    b, s, d = x.shape
    heads = a_log.shape[0]
    dk = GDN_K_DIM
    dv = norm_gain.shape[0]
    n_main = 2 * heads * dk + 2 * heads * dv
    w_ab = jnp.zeros((d, LANES), BF16).at[:, :2 * heads].set(w_in[:, n_main:].astype(BF16))
    pad = lambda p: jnp.zeros((1, LANES), F32).at[0, :heads].set(p)
    proj, bg = _gdn_proj(x.reshape(b * s, d), gain, w_in[:, :n_main].astype(BF16), w_ab, conv_w,
                         pad(a_log), pad(dt_bias), seq=s, heads=heads, dk=dk)
    return _gdn_core(proj.reshape(b, s, -1), bg.reshape(b, s, LANES), norm_gain, heads=heads, dk=dk, dv=dv)


def _ffn_layer(x, gain, w_in, w_out, o=None, w_o=None):
    b, s, d = x.shape
    if o is not None:
        o, w_o = o.reshape(b * s, -1), w_o.astype(BF16)
    return _ffn(x.reshape(b * s, d), gain, w_in.astype(BF16), w_out.astype(BF16), o, w_o).reshape(b, s, d)


def kernel(x, norm_mix, norm_ffn, ffn_w_in, ffn_w_out, pool_w, pool_scale, sb_w_qkv, sb_q_gain, sb_k_gain, sb_w_o, ret_w_in, ret_gn_gain, ret_w_o, gdn_w_in, gdn_conv_w, gdn_a_log, gdn_dt_bias, gdn_norm_gain, gdn_w_o):
    depth = norm_mix.shape[0]
    for layer in range(depth):
        mixer, occ = layer % 4, layer // 4
        o, w_o = None, None
        if mixer == 0:
            x = _pool_layer(x, norm_mix[layer], pool_w[occ].astype(BF16), pool_scale[occ])
        elif mixer == 1:
            o, w_o = _sb_mixer(x, norm_mix[layer], sb_w_qkv[occ], sb_q_gain[occ], sb_k_gain[occ]), sb_w_o[occ]
        elif mixer == 2:
            o, w_o = _ret_mixer(x, norm_mix[layer], ret_w_in[occ], ret_gn_gain[occ]), ret_w_o[occ]
        else:
            o, w_o = _gdn_mixer(x, norm_mix[layer], gdn_w_in[occ], gdn_conv_w[occ], gdn_a_log[occ],
                                gdn_dt_bias[occ], gdn_norm_gain[occ]), gdn_w_o[occ]
        x = _ffn_layer(x, norm_ffn[layer], ffn_w_in[layer], ffn_w_out[layer], o, w_o)
    return x
```

```python
import functools

import jax
import jax.numpy as jnp
from jax import lax
from jax.experimental import pallas as pl
from jax.experimental.pallas import tpu as pltpu

F32 = jnp.float32
BF16 = jnp.bfloat16

EPS = 1e-6
LOG2E = 1.4426950408889634
SOFTPLUS_LINEAR = 64.0
UNDERFLOW_LOG2 = 160.0
ROPE_BASE = 10000.0
LANES = 128
SUBLANES = 8
VMEM_LIMIT = 56 * 1024 * 1024
PROJ_TILE = 256

POOL_WINDOWS = (2, 4, 8, 16)
POOL_HALO = 32
SB_HEAD_DIM = 128
RET_HEADS = 4
RET_CHUNK = 256
GDN_K_DIM = 128
GDN_CONV = 4
GDN_CHUNK = 64
CONV_ROWS = 64


def _params(*sem):
    return pltpu.CompilerParams(dimension_semantics=sem, vmem_limit_bytes=VMEM_LIMIT)


def _resident(shape):
    return pl.BlockSpec(shape, lambda *_: (0,) * len(shape), pipeline_mode=pl.Buffered(1))


def _rms(xf, gain_row):
    ms = jnp.mean(xf * xf, axis=-1, keepdims=True)
    return xf * lax.rsqrt(ms + EPS) * gain_row


def _silu(x):
    h = 0.5 * x
    return h + h * jnp.tanh(h)


def _dot(a, b):
    return jnp.dot(a, b, preferred_element_type=F32)


def _dot_nt(a, b):
    return lax.dot_general(a, b, (((1,), (1,)), ((), ())), preferred_element_type=F32)


def _dot_tn(a, b):
    return lax.dot_general(a, b, (((0,), (0,)), ((), ())), preferred_element_type=F32)


def _ffn_kernel(*refs, mixer_proj):
    if mixer_proj:
        x_ref, a_ref, wo_ref, g_ref, wi_ref, wd_ref, o_ref, h_scr, act_scr, x1_scr = refs
        x1_scr[...] = x_ref[...] + _dot(a_ref[...], wo_ref[...])
        x1_ref = x1_scr
    else:
        x_ref, g_ref, wi_ref, wd_ref, o_ref, h_scr, act_scr = refs
        x1_ref = x_ref
    h_scr[...] = _rms(x1_ref[...], g_ref[...]).astype(BF16)
    hidden = wd_ref.shape[0]
    for f0 in range(0, hidden, PROJ_TILE):
        gate = _dot(h_scr[...], wi_ref[:, f0:f0 + PROJ_TILE])
        up = _dot(h_scr[...], wi_ref[:, hidden + f0:hidden + f0 + PROJ_TILE])
        act_scr[:, f0:f0 + PROJ_TILE] = (_silu(gate) * up).astype(BF16)
    for c0 in range(0, o_ref.shape[1], PROJ_TILE):
        cols = slice(c0, c0 + PROJ_TILE)
        o_ref[:, cols] = x1_ref[:, cols] + _dot(act_scr[...], wd_ref[:, cols])


def _ffn(x2, gain, w_in, w_out, a2=None, w_o=None, *, tm=512):
    t, d = x2.shape
    hidden = w_out.shape[0]
    assert t % tm == 0 and hidden % PROJ_TILE == 0 and d % PROJ_TILE == 0
    row = lambda width: pl.BlockSpec((tm, width), lambda i: (i, 0))
    mixer_proj = a2 is not None
    in_specs = [row(d)]
    args = [x2]
    scratch = [pltpu.VMEM((tm, d), BF16), pltpu.VMEM((tm, hidden), BF16)]
    if mixer_proj:
        in_specs += [row(a2.shape[1]), _resident(w_o.shape)]
        args += [a2, w_o]
        scratch.append(pltpu.VMEM((tm, d), F32))
    in_specs += [_resident((1, d)), _resident(w_in.shape), _resident(w_out.shape)]
    args += [gain.reshape(1, d), w_in, w_out]
    return pl.pallas_call(
        functools.partial(_ffn_kernel, mixer_proj=mixer_proj),
        grid=(t // tm,),
        in_specs=in_specs,
        out_specs=row(d),
        out_shape=jax.ShapeDtypeStruct((t, d), F32),
        scratch_shapes=scratch,
        compiler_params=_params("parallel"),
        name="ffn",
    )(*args)


def _pool_kernel(x_ref, g_ref, w_ref, sc_ref, o_ref, ext, buf_a, buf_b, *, ts, group):
    s = pl.program_id(1)
    n = ts + POOL_HALO
    d = x_ref.shape[-1]

    @pl.when(s == 0)
    def _():
        ext[0:POOL_HALO, :] = jnp.zeros((POOL_HALO, d), F32)

    x = x_ref[...]
    ext[POOL_HALO:n, :] = _rms(x, g_ref[...])

    src, dst = ext, buf_a
    for level in range(1, len(POOL_WINDOWS) + 1):
        shift = 1 << (level - 1)
        lo = SUBLANES * level
        c0 = (level - 1) * group
        dst[lo:n, c0:d] = src[lo:n, c0:d] + src[lo - shift:n - shift, c0:d]
        src, dst = dst, (buf_b if dst is buf_a else buf_a)

    pos1 = (s * ts + lax.broadcasted_iota(jnp.int32, (ts, 1), 0) + 1).astype(F32)
    for g, win in enumerate(POOL_WINDOWS):
        level_buf = buf_a if g % 2 == 0 else buf_b
        cols = slice(g * group, (g + 1) * group)
        inv_cnt = 1.0 / jnp.minimum(pos1, float(win))
        pooled = level_buf[POOL_HALO:n, cols] * inv_cnt - ext[POOL_HALO:n, cols]
        mixed = _dot(pooled.astype(BF16), w_ref[g])
        o_ref[:, cols] = x[:, cols] + mixed * sc_ref[:, cols]

    ext[0:POOL_HALO, :] = ext[ts:n, :]


def _pool_layer(x, gain, w_group, scale, *, ts=512):
    b, s, d = x.shape
    ngroups, group, _ = w_group.shape
    assert s % ts == 0 and ngroups == len(POOL_WINDOWS) and ngroups * group == d
    n = ts + POOL_HALO
    return pl.pallas_call(
        functools.partial(_pool_kernel, ts=ts, group=group),
        grid=(b, s // ts),
        in_specs=[
            pl.BlockSpec((None, ts, d), lambda i, j: (i, j, 0)),
            pl.BlockSpec((1, d), lambda i, j: (0, 0)),
            pl.BlockSpec((ngroups, group, group), lambda i, j: (0, 0, 0)),
            pl.BlockSpec((1, d), lambda i, j: (0, 0)),
        ],
        out_specs=pl.BlockSpec((None, ts, d), lambda i, j: (i, j, 0)),
        out_shape=jax.ShapeDtypeStruct((b, s, d), F32),
        scratch_shapes=[pltpu.VMEM((n, d), F32)] * 3,
        compiler_params=_params("arbitrary", "arbitrary"),
        name="pool_layer",
    )(x, gain.reshape(1, d), w_group, scale.reshape(1, d))


def _sb_proj_kernel(x_ref, g_ref, w_ref, qg_ref, kg_ref, o_ref, h_scr, *, width):
    h_scr[...] = _rms(x_ref[...], g_ref[...]).astype(BF16)
    gains = (qg_ref[...] * (SB_HEAD_DIM ** -0.5 * LOG2E), kg_ref[...], None)
    for tile in range(w_ref.shape[1] // PROJ_TILE):
        cols = slice(tile * PROJ_TILE, (tile + 1) * PROJ_TILE)
        y = _dot(h_scr[...], w_ref[:, cols])
        gain = gains[tile * PROJ_TILE // width]
        if gain is None:
            o_ref[:, cols] = y.astype(o_ref.dtype)
        else:
            for c in range(PROJ_TILE // SB_HEAD_DIM):
                sub = slice(c * SB_HEAD_DIM, (c + 1) * SB_HEAD_DIM)
                o_ref[:, tile * PROJ_TILE + c * SB_HEAD_DIM:tile * PROJ_TILE + (c + 1) * SB_HEAD_DIM] = (
                    _rms(y[:, sub], gain).astype(o_ref.dtype))


def _sb_proj(x2, gain, w_qkv, q_gain, k_gain, *, tm=512):
    t, d = x2.shape
    n = w_qkv.shape[1]
    width = n // 3
    assert t % tm == 0 and width % PROJ_TILE == 0 and PROJ_TILE % SB_HEAD_DIM == 0
    return pl.pallas_call(
        functools.partial(_sb_proj_kernel, width=width),
        grid=(t // tm,),
        in_specs=[
            pl.BlockSpec((tm, d), lambda i: (i, 0)),
            _resident((1, d)),
            _resident((d, n)),
            _resident((1, SB_HEAD_DIM)),
            _resident((1, SB_HEAD_DIM)),
        ],
        out_specs=pl.BlockSpec((tm, n), lambda i: (i, 0)),
        out_shape=jax.ShapeDtypeStruct((t, n), BF16),
        scratch_shapes=[pltpu.VMEM((tm, d), BF16)],
        compiler_params=_params("parallel"),
        name="sb_proj",
    )(x2, gain.reshape(1, d), w_qkv, q_gain.reshape(1, -1), k_gain.reshape(1, -1))


def _sb_attn_kernel(q_ref, k_ref, v_ref, o_ref, *, tq, unroll):
    i = pl.program_id(2)
    tk = tq
    dh = SB_HEAD_DIM
    hp = q_ref.shape[1] // dh
    lanes = [slice(h * dh, (h + 1) * dh) for h in range(hp)]
    q = [q_ref[:, ln] for ln in lanes]
    row = lax.broadcasted_iota(jnp.int32, (tq, tk), 0)
    col = lax.broadcasted_iota(jnp.int32, (tq, tk), 1)
    causal = col < row
    jj = lax.broadcasted_iota(jnp.int32, (tk, tk), 0)
    ss = lax.broadcasted_iota(jnp.int32, (tk, tk), 1)
    suffix = jnp.where(jj > ss, 1.0, 0.0).astype(BF16)

    def scores(h, kb, keep):
        start = pl.multiple_of(kb * tk, tk)
        z = _dot_nt(q[h], k_ref[pl.ds(start, tk), lanes[h]])
        p = jnp.where(z > SOFTPLUS_LINEAR, z, jnp.log(1.0 + jnp.exp2(z)) * LOG2E)
        log_beta = z - p
        if keep is not None:
            p = jnp.where(keep, p, 0.0)
        return p.astype(BF16), log_beta, p[:, :1]

    def blocks(kbs, keeps, acc, run):
        acc, run = list(acc), list(run)
        stage1 = [[scores(h, kb, keep) for kb, keep in zip(kbs, keeps)] for h in range(hp)]
        afters = [[_dot(p_bf, suffix) for p_bf, _, _ in per_head] for per_head in stage1]
        for n, (kb, keep) in enumerate(zip(kbs, keeps)):
            start = pl.multiple_of(kb * tk, tk)
            for h in range(hp):
                _, log_beta, first_col = stage1[h][n]
                after = afters[h][n]
                wts = jnp.exp2(log_beta - (after + run[h]))
                if keep is not None:
                    wts = jnp.where(keep, wts, 0.0)
                acc[h] = acc[h] + _dot(wts.astype(BF16), v_ref[pl.ds(start, tk), lanes[h]])
                run[h] = run[h] + (after[:, :1] + first_col)
        return tuple(acc), tuple(run)

    def live(run):
        return functools.reduce(jnp.minimum, [jnp.min(r) for r in run]) < UNDERFLOW_LOG2

    def head(n_extra):
        def branch():
            acc = tuple(jnp.zeros((tq, dh), F32) for _ in range(hp))
            run = tuple(jnp.zeros((tq, 1), F32) for _ in range(hp))
            kbs = [i] + [i - 1 - u for u in range(n_extra)]
            return blocks(kbs, [causal] + [None] * n_extra, acc, run)
        return branch

    n_groups = (jnp.maximum(i - 1, 0) + unroll - 1) // unroll

    def group(carry):
        t, acc, run, _ = carry
        kbs = [i - 2 - unroll * t - u for u in range(unroll)]
        acc, run = blocks([jnp.maximum(kb, 0) for kb in kbs], [kb >= 0 for kb in kbs], acc, run)
        return t + 1, acc, run, live(run)

    acc, run = lax.switch(jnp.minimum(i, 1), [head(0), head(1)])
    _, acc, run, _ = lax.while_loop(lambda c: (c[0] < n_groups) & c[3], group,
                                    (jnp.int32(0), acc, run, live(run)))
    for h in range(hp):
        o_ref[:, lanes[h]] = acc[h].astype(o_ref.dtype)


def _sb_attn(qkv, *, heads, tq=256, unroll=2, hp=4):
    b, s, _ = qkv.shape
    width = hp * SB_HEAD_DIM
    nhb = heads // hp
    assert s % tq == 0 and heads % hp == 0
    return pl.pallas_call(
        functools.partial(_sb_attn_kernel, tq=tq, unroll=unroll),
        grid=(b, nhb, s // tq),
        in_specs=[
            pl.BlockSpec((None, tq, width), lambda bi, h, i: (bi, i, h)),
            pl.BlockSpec((None, s, width), lambda bi, h, i: (bi, 0, nhb + h)),
            pl.BlockSpec((None, s, width), lambda bi, h, i: (bi, 0, 2 * nhb + h)),
        ],
        out_specs=pl.BlockSpec((None, tq, width), lambda bi, h, i: (bi, i, h)),
        out_shape=jax.ShapeDtypeStruct((b, s, heads * SB_HEAD_DIM), BF16),
        compiler_params=_params("parallel", "parallel", "arbitrary"),
        name="sb_attn",
    )(qkv, qkv, qkv)


def _ret_proj_kernel(x_ref, g_ref, w_ref, cos_ref, sin_ref, o_ref, h_scr, *, dk, heads):
    h_scr[...] = _rms(x_ref[...], g_ref[...]).astype(BF16)
    half = dk // 2
    cos = cos_ref[...]
    sin = sin_ref[...]
    for tile in range(w_ref.shape[1] // dk):
        c0 = tile * dk
        y = _dot(h_scr[...], w_ref[:, c0:c0 + dk])
        if tile < 2 * heads:
            if tile >= heads:
                y = y * (dk ** -0.5)
            x1 = y[:, :half]
            x2 = y[:, half:]
            o_ref[:, c0:c0 + half] = (x1 * cos - x2 * sin).astype(o_ref.dtype)
            o_ref[:, c0 + half:c0 + dk] = (x1 * sin + x2 * cos).astype(o_ref.dtype)
        else:
            o_ref[:, c0:c0 + dk] = y.astype(o_ref.dtype)


def _ret_proj(x2, gain, w_in, cos, sin, *, seq, dk, heads, tm=512):
    t, d = x2.shape
    n = w_in.shape[1]
    assert t % tm == 0 and seq % tm == 0 and n % dk == 0 and dk == PROJ_TILE
    per_seq = seq // tm
    return pl.pallas_call(
        functools.partial(_ret_proj_kernel, dk=dk, heads=heads),
        grid=(t // tm,),
        in_specs=[
            pl.BlockSpec((tm, d), lambda i: (i, 0)),
            _resident((1, d)),
            _resident((d, n)),
            pl.BlockSpec((tm, dk // 2), lambda i: (i % per_seq, 0)),
            pl.BlockSpec((tm, dk // 2), lambda i: (i % per_seq, 0)),
        ],
        out_specs=pl.BlockSpec((tm, n), lambda i: (i, 0)),
        out_shape=jax.ShapeDtypeStruct((t, n), BF16),
        scratch_shapes=[pltpu.VMEM((tm, d), BF16)],
        compiler_params=_params("parallel"),
        name="ret_proj",
    )(x2, gain.reshape(1, d), w_in, cos, sin)


def _ret_kernel(lg_ref, q_ref, k_ref, v_ref, g_ref, gn_ref, o_ref, state, *, rows, chunk, heads):
    c = pl.program_id(1)
    dk = q_ref.shape[1] // heads
    dv = v_ref.shape[1] // heads

    @pl.when(c == 0)
    def _():
        state[...] = jnp.zeros_like(state)

    ri = lax.broadcasted_iota(jnp.int32, (chunk, chunk), 0)
    ci = lax.broadcasted_iota(jnp.int32, (chunk, chunk), 1)
    diff = (ri - ci).astype(F32)
    idx = lax.broadcasted_iota(jnp.int32, (chunk, 1), 0).astype(F32)
    lg = [lg_ref[h][:, :1] for h in range(heads)]
    intra = [jnp.where(diff >= 0, jnp.exp(jnp.maximum(diff, 0.0) * l), 0.0) for l in lg]
    q_dec = [jnp.exp((idx + 1.0) * l) for l in lg]
    k_dec = [jnp.exp((chunk - 1.0 - idx) * l) for l in lg]
    chunk_dec = [jnp.exp(chunk * l) for l in lg]

    st = [state[h] for h in range(heads)]
    for n in range(rows // chunk):
        r = slice(n * chunk, (n + 1) * chunk)
        q = [q_ref[r, h * dk:(h + 1) * dk] for h in range(heads)]
        k = [k_ref[r, h * dk:(h + 1) * dk] for h in range(heads)]
        v = [v_ref[r, h * dv:(h + 1) * dv] for h in range(heads)]
        scores = [(_dot_nt(a, b) * m).astype(BF16) for a, b, m in zip(q, k, intra)]
        kd = [(b.astype(F32) * d).astype(BF16) for b, d in zip(k, k_dec)]
        inter = [_dot(a, s.astype(BF16)) for a, s in zip(q, st)]
        kv = [_dot_tn(a, b) for a, b in zip(kd, v)]
        o = [_dot(sc, b) + d * it for sc, b, d, it in zip(scores, v, q_dec, inter)]
        st = [s * d + u for s, d, u in zip(st, chunk_dec, kv)]
        for h in range(heads):
            gate = g_ref[r, h * dv:(h + 1) * dv].astype(F32)
            o_ref[r, h * dv:(h + 1) * dv] = (_rms(o[h], gn_ref[...]) * _silu(gate)).astype(o_ref.dtype)
    for h in range(heads):
        state[h] = st[h]


def _ret_core(proj, lg_rows, gn_gain, *, heads, dk, dv, rows=512):
    b, s, _ = proj.shape
    assert s % rows == 0 and rows % RET_CHUNK == 0 and dv == 2 * dk
    return pl.pallas_call(
        functools.partial(_ret_kernel, rows=rows, chunk=RET_CHUNK, heads=heads),
        grid=(b, s // rows),
        in_specs=[
            _resident((heads, 1, LANES)),
            pl.BlockSpec((None, rows, heads * dk), lambda bi, c: (bi, c, 0)),
            pl.BlockSpec((None, rows, heads * dk), lambda bi, c: (bi, c, 1)),
            pl.BlockSpec((None, rows, heads * dv), lambda bi, c: (bi, c, 1)),
            pl.BlockSpec((None, rows, heads * dv), lambda bi, c: (bi, c, 2)),
            _resident((1, dv)),
        ],
        out_specs=pl.BlockSpec((None, rows, heads * dv), lambda bi, c: (bi, c, 0)),
        out_shape=jax.ShapeDtypeStruct((b, s, heads * dv), BF16),
        scratch_shapes=[pltpu.VMEM((heads, dk, dv), F32)],
        compiler_params=_params("parallel", "arbitrary"),
        name="ret_core",
    )(lg_rows, proj, proj, proj, proj, gn_gain.reshape(1, dv))


def _gdn_proj_kernel(x_ref, g_ref, w_ref, wab_ref, cw_ref, alog_ref, dtb_ref, o_ref, bg_ref,
                     h_scr, halo, *, per_seq, heads, dk):
    i = pl.program_id(0)
    tm = x_ref.shape[0]
    n_conv = cw_ref.shape[1]

    @pl.when(i == 0)
    def _():
        halo[...] = jnp.zeros_like(halo)

    h_scr[...] = _rms(x_ref[...], g_ref[...]).astype(BF16)
    first = (i % per_seq) == 0
    row8 = lax.broadcasted_iota(jnp.int32, (SUBLANES, PROJ_TILE), 0)
    conv_tiles = list(range(n_conv // PROJ_TILE))
    plain_tiles = list(range(n_conv // PROJ_TILE, w_ref.shape[1] // PROJ_TILE))
    per_plain = -(-len(conv_tiles) // max(len(plain_tiles), 1))
    order = []
    for n, tile in enumerate(conv_tiles):
        order.append(tile)
        if (n + 1) % per_plain == 0 and plain_tiles:
            order.append(plain_tiles.pop(0))
    order += plain_tiles
    for tile in order:
        c0 = tile * PROJ_TILE
        cols = slice(c0, c0 + PROJ_TILE)
        y = _dot(h_scr[...], w_ref[:, cols])
        if c0 >= n_conv:
            o_ref[:, cols] = y.astype(o_ref.dtype)
            continue
        prev = jnp.where(first, 0.0, halo[:, cols])
        halo[:, cols] = y[tm - SUBLANES:tm, :]
        for r0 in range(0, tm, CONV_ROWS):
            rows = slice(r0, r0 + CONV_ROWS)
            yc = y[rows]
            conv = cw_ref[GDN_CONV - 1:GDN_CONV, cols] * yc
            for back in range(1, GDN_CONV):
                shifted = pltpu.roll(yc, back, axis=0)
                top = jnp.where(row8 < back, pltpu.roll(prev, back, axis=0), shifted[:SUBLANES])
                shifted = jnp.concatenate([top, shifted[SUBLANES:]], axis=0)
                conv = conv + cw_ref[GDN_CONV - 1 - back:GDN_CONV - back, cols] * shifted
            prev = yc[CONV_ROWS - SUBLANES:]
            act = _silu(conv)
            if c0 >= 2 * heads * dk:
                o_ref[rows, cols] = act.astype(o_ref.dtype)
                continue
            scale = dk ** -0.5 if c0 < heads * dk else 1.0
            for c in range(PROJ_TILE // dk):
                a = act[:, c * dk:(c + 1) * dk]
                nrm = lax.rsqrt(jnp.sum(a * a, axis=-1, keepdims=True) + EPS) * scale
                o_ref[rows, c0 + c * dk:c0 + (c + 1) * dk] = (a * nrm).astype(o_ref.dtype)

    y = _dot(h_scr[...], wab_ref[...])
    a = y + dtb_ref[...]
    softplus = jnp.maximum(a, 0.0) + jnp.log(1.0 + jnp.exp(-jnp.abs(a)))
    log_decay = -jnp.exp(alog_ref[...]) * softplus
    beta = jax.nn.sigmoid(y)
    lane = lax.broadcasted_iota(jnp.int32, y.shape, 1)
    bg_ref[...] = jnp.where(lane < heads, log_decay, beta)


def _gdn_proj(x2, gain, w_main, w_ab, conv_w, a_log_row, dt_bias_row, *, seq, heads, dk, tm=512):
    t, d = x2.shape
    n = w_main.shape[1]
    n_conv = conv_w.shape[1]
    assert t % tm == 0 and seq % tm == 0 and n % PROJ_TILE == 0 and n_conv % PROJ_TILE == 0
    assert PROJ_TILE % dk == 0 and (heads * dk) % PROJ_TILE == 0
    per_seq = seq // tm
    return pl.pallas_call(
        functools.partial(_gdn_proj_kernel, per_seq=per_seq, heads=heads, dk=dk),
        grid=(t // tm,),
        in_specs=[
            pl.BlockSpec((tm, d), lambda i: (i, 0)),
            _resident((1, d)),
            _resident((d, n)),
            _resident((d, LANES)),
            _resident((GDN_CONV, n_conv)),
            _resident((1, LANES)),
            _resident((1, LANES)),
        ],
        out_specs=[pl.BlockSpec((tm, n), lambda i: (i, 0)), pl.BlockSpec((tm, LANES), lambda i: (i, 0))],
        out_shape=[jax.ShapeDtypeStruct((t, n), BF16), jax.ShapeDtypeStruct((t, LANES), F32)],
        scratch_shapes=[
            pltpu.VMEM((tm, d), BF16),
            pltpu.VMEM((SUBLANES, n_conv), F32),
        ],
        compiler_params=_params("arbitrary"),
        name="gdn_proj",
    )(x2, gain.reshape(1, d), w_main, w_ab, conv_w, a_log_row, dt_bias_row)


def _gdn_kernel(q_ref, k_ref, v_ref, z_ref, bg_ref, gain_ref, o_ref, state, *, rows, chunk, heads, hp):
    hb = pl.program_id(1)
    c = pl.program_id(2)
    C = chunk
    nc = rows // C
    dk = q_ref.shape[1] // hp
    dv = v_ref.shape[1] // hp

    @pl.when(c == 0)
    def _():
        state[...] = jnp.zeros_like(state)

    lane = lax.broadcasted_iota(jnp.int32, (C, LANES), 1)
    ri = lax.broadcasted_iota(jnp.int32, (C, C), 0)
    ci = lax.broadcasted_iota(jnp.int32, (C, C), 1)
    tril = ci <= ri
    strict = ci < ri
    eye = jnp.where(ci == ri, 1.0, 0.0)
    ai = lax.broadcasted_iota(jnp.int32, (2 * C, C), 0)
    am = lax.broadcasted_iota(jnp.int32, (2 * C, C), 1)
    sum_sel = jnp.where(((ai < C) & (am <= ai)) | ((ai >= C) & (am > ai - C)), 1.0, 0.0).astype(BF16)
    sum_sel2 = jnp.concatenate([sum_sel, sum_sel], axis=1)
    bm = lax.broadcasted_iota(jnp.int32, (C, 2 * C), 0)
    bj = lax.broadcasted_iota(jnp.int32, (C, 2 * C), 1)
    col_sel = jnp.where((bm > bj) | (bj >= C), 1.0, 0.0)

    items = [(n, h) for n in range(nc) for h in range(hp)]
    r_of = lambda n: slice(n * C, (n + 1) * C)
    c_of = lambda h, w: slice(h * w, (h + 1) * w)
    bgs = [bg_ref[r_of(n), :] for n in range(nc)]
    pick = lambda n, ln: jnp.sum(jnp.where(lane == ln, bgs[n], 0.0), axis=-1, keepdims=True)
    g_col = [pick(n, hb * hp + h) for n, h in items]
    b_col = [pick(n, hb * hp + h + heads) for n, h in items]

    gmat = [g * col_sel for g in g_col]
    g_hi = [m.astype(BF16) for m in gmat]
    g_split = [jnp.concatenate([hi, (m - hi.astype(F32)).astype(BF16)], axis=0) for m, hi in zip(gmat, g_hi)]
    sums = [_dot(sum_sel2, m) for m in g_split]
    decay = [jnp.where(tril, jnp.exp(s[:C, :C]), 0.0) for s in sums]
    g_cum = [s[:C, C:C + 1] for s in sums]
    exp_g = [jnp.exp(g) for g in g_cum]
    exp_rem = [jnp.exp(s[C:, C:C + 1]) for s in sums]
    exp_last = [e[C - 1:C, :] for e in exp_g]

    q = [q_ref[r_of(n), c_of(h, dk)] for n, h in items]
    k = [k_ref[r_of(n), c_of(h, dk)] for n, h in items]
    kf = [x.astype(F32) for x in k]
    k_beta = [x * b for x, b in zip(kf, b_col)]
    kq = [_dot_nt(jnp.concatenate([kb.astype(BF16), a], axis=0), x) for kb, a, x in zip(k_beta, q, k)]
    n_mat = [jnp.where(strict, m[:C] * d, 0.0) for m, d in zip(kq, decay)]
    attn = [jnp.where(tril, m[C:] * d, 0.0).astype(BF16) for m, d in zip(kq, decay)]
    inv = [eye - m for m in n_mat]
    n_bf = [m.astype(BF16) for m in n_mat]
    power = [_dot(m, m) for m in n_bf]
    for _ in range(4):
        p_bf = [p.astype(BF16) for p in power]
        both = [_dot(jnp.concatenate([x.astype(BF16), p], axis=0), p) for x, p in zip(inv, p_bf)]
        inv = [x + m[:C] for x, m in zip(inv, both)]
        power = [m[C:] for m in both]
    inv = [x + _dot(x.astype(BF16), p.astype(BF16)) for x, p in zip(inv, power)]

    rhs = [jnp.concatenate([v_ref[r_of(n), c_of(h, dv)].astype(F32) * b, kb * eg], axis=1).astype(BF16)
           for (n, h), b, kb, eg in zip(items, b_col, k_beta, exp_g)]
    uw = [_dot(x.astype(BF16), r) for x, r in zip(inv, rhs)]
    kd = [(x * e).astype(BF16) for x, e in zip(kf, exp_rem)]

    st = [state[h] for h in range(hp)]
    for n in range(nc):
        idx = [n * hp + h for h in range(hp)]
        wq = [jnp.concatenate([uw[i][:, dv:].astype(BF16), q[i]], axis=0) for i in idx]
        ws = [_dot(a, s.astype(BF16)) for a, s in zip(wq, st)]
        v_new = [(uw[i][:, :dv] - w[:C]).astype(BF16) for i, w in zip(idx, ws)]
        o = [exp_g[i] * w[C:] + _dot(attn[i], vn) for i, w, vn in zip(idx, ws, v_new)]
        st = [s * exp_last[i] + _dot_tn(kd[i], vn) for i, s, vn in zip(idx, st, v_new)]
        for h in range(hp):
            gate = z_ref[r_of(n), c_of(h, dv)].astype(F32)
            o_ref[r_of(n), c_of(h, dv)] = (_rms(o[h], gain_ref[...]) * _silu(gate)).astype(o_ref.dtype)
    for h in range(hp):
        state[h] = st[h]


def _gdn_core(proj, bg, norm_gain, *, heads, dk, dv, rows=512, hp=8):
    b, s, _ = proj.shape
    assert s % rows == 0 and rows % GDN_CHUNK == 0 and dv == 2 * dk and GDN_CHUNK == 64
    assert heads % hp == 0
    nhb = heads // hp
    return pl.pallas_call(
        functools.partial(_gdn_kernel, rows=rows, chunk=GDN_CHUNK, heads=heads, hp=hp),
        grid=(b, nhb, s // rows),
        in_specs=[
            pl.BlockSpec((None, rows, hp * dk), lambda bi, h, c: (bi, c, h)),
            pl.BlockSpec((None, rows, hp * dk), lambda bi, h, c: (bi, c, nhb + h)),
            pl.BlockSpec((None, rows, hp * dv), lambda bi, h, c: (bi, c, nhb + h)),
            pl.BlockSpec((None, rows, hp * dv), lambda bi, h, c: (bi, c, 2 * nhb + h)),
            pl.BlockSpec((None, rows, LANES), lambda bi, h, c: (bi, c, 0)),
            pl.BlockSpec((1, dv), lambda bi, h, c: (0, 0)),
        ],
        out_specs=pl.BlockSpec((None, rows, hp * dv), lambda bi, h, c: (bi, c, h)),
        out_shape=jax.ShapeDtypeStruct((b, s, heads * dv), BF16),
        scratch_shapes=[pltpu.VMEM((hp, dk, dv), F32)],
        compiler_params=_params("parallel", "parallel", "arbitrary"),
        name="gdn_core",
    )(proj, proj, proj, proj, bg, norm_gain.reshape(1, dv))


def _sb_mixer(x, gain, w_qkv, q_gain, k_gain):
    b, s, d = x.shape
    heads = w_qkv.shape[1] // (3 * SB_HEAD_DIM)
    qkv = _sb_proj(x.reshape(b * s, d), gain, w_qkv.astype(BF16), q_gain, k_gain)
    return _sb_attn(qkv.reshape(b, s, -1), heads=heads)


def _ret_mixer(x, gain, w_in, gn_gain):
    b, s, d = x.shape
    heads = RET_HEADS
    dv = gn_gain.shape[0]
    dk = dv // 2
    half = dk // 2
    inv = ROPE_BASE ** (-jnp.arange(half, dtype=F32) / half)
    ang = jnp.arange(s, dtype=F32)[:, None] * inv[None, :]
    log_gamma = jnp.log1p(-jnp.exp2(-5.0 - jnp.arange(heads, dtype=F32)))
    lg_rows = jnp.broadcast_to(log_gamma[:, None, None], (heads, 1, LANES))
    proj = _ret_proj(x.reshape(b * s, d), gain, w_in.astype(BF16), jnp.cos(ang), jnp.sin(ang),
                     seq=s, dk=dk, heads=heads)
    return _ret_core(proj.reshape(b, s, -1), lg_rows, gn_gain, heads=heads, dk=dk, dv=dv)


def _gdn_mixer(x, gain, w_in, conv_w, a_log, dt_bias, norm_gain):
    b, s, d = x.shape
    heads = a_log.shape[0]
    dk = GDN_K_DIM
    dv = norm_gain.shape[0]
    n_main = 2 * heads * dk + 2 * heads * dv
    w_ab = jnp.zeros((d, LANES), BF16).at[:, :2 * heads].set(w_in[:, n_main:].astype(BF16))
    pad = lambda p: jnp.zeros((1, LANES), F32).at[0, :heads].set(p)
    proj, bg = _gdn_proj(x.reshape(b * s, d), gain, w_in[:, :n_main].astype(BF16), w_ab, conv_w,
                         pad(a_log), pad(dt_bias), seq=s, heads=heads, dk=dk)
    return _gdn_core(proj.reshape(b, s, -1), bg.reshape(b, s, LANES), norm_gain, heads=heads, dk=dk, dv=dv)


def _ffn_layer(x, gain, w_in, w_out, o=None, w_o=None):
    b, s, d = x.shape
    if o is not None:
        o, w_o = o.reshape(b * s, -1), w_o.astype(BF16)
    return _ffn(x.reshape(b * s, d), gain, w_in.astype(BF16), w_out.astype(BF16), o, w_o).reshape(b, s, d)


def kernel(x, norm_mix, norm_ffn, ffn_w_in, ffn_w_out, pool_w, pool_scale, sb_w_qkv, sb_q_gain, sb_k_gain, sb_w_o, ret_w_in, ret_gn_gain, ret_w_o, gdn_w_in, gdn_conv_w, gdn_a_log, gdn_dt_bias, gdn_norm_gain, gdn_w_o):
    depth = norm_mix.shape[0]
    for layer in range(depth):
        mixer, occ = layer % 4, layer // 4
        o, w_o = None, None
        if mixer == 0:
            x = _pool_layer(x, norm_mix[layer], pool_w[occ].astype(BF16), pool_scale[occ])
        elif mixer == 1:
            o, w_o = _sb_mixer(x, norm_mix[layer], sb_w_qkv[occ], sb_q_gain[occ], sb_k_gain[occ]), sb_w_o[occ]
        elif mixer == 2:
            o, w_o = _ret_mixer(x, norm_mix[layer], ret_w_in[occ], ret_gn_gain[occ]), ret_w_o[occ]
        else:
            o, w_o = _gdn_mixer(x, norm_mix[layer], gdn_w_in[occ], gdn_conv_w[occ], gdn_a_log[occ],
                                gdn_dt_bias[occ], gdn_norm_gain[occ]), gdn_w_o[occ]
        x = _ffn_layer(x, norm_ffn[layer], ffn_w_in[layer], ffn_w_out[layer], o, w_o)
    return x
```

```python
import functools

import jax
import jax.numpy as jnp
from jax import lax
from jax.experimental import pallas as pl
from jax.experimental.pallas import tpu as pltpu

F32 = jnp.float32
BF16 = jnp.bfloat16

EPS = 1e-6
LOG2E = 1.4426950408889634
SOFTPLUS_LINEAR = 64.0
UNDERFLOW_LOG2 = 160.0
ROPE_BASE = 10000.0
LANES = 128
SUBLANES = 8
VMEM_LIMIT = 56 * 1024 * 1024
PROJ_TILE = 256

POOL_WINDOWS = (2, 4, 8, 16)
POOL_HALO = 32
SB_HEAD_DIM = 128
RET_HEADS = 4
RET_CHUNK = 256
GDN_K_DIM = 128
GDN_CONV = 4
GDN_CHUNK = 64
CONV_ROWS = 64


def _params(*sem):
    return pltpu.CompilerParams(dimension_semantics=sem, vmem_limit_bytes=VMEM_LIMIT)


def _resident(shape):
    return pl.BlockSpec(shape, lambda *_: (0,) * len(shape), pipeline_mode=pl.Buffered(1))


def _rms(xf, gain_row):
    ms = jnp.mean(xf * xf, axis=-1, keepdims=True)
    return xf * lax.rsqrt(ms + EPS) * gain_row


def _silu(x):
    h = 0.5 * x
    return h + h * jnp.tanh(h)


def _dot(a, b):
    return jnp.dot(a, b, preferred_element_type=F32)


def _dot_nt(a, b):
    return lax.dot_general(a, b, (((1,), (1,)), ((), ())), preferred_element_type=F32)


def _dot_tn(a, b):
    return lax.dot_general(a, b, (((0,), (0,)), ((), ())), preferred_element_type=F32)


def _pool_residual(x, seq_tile, g_ref, w_ref, sc_ref, ext, buf_a, buf_b, dst_ref):
    ts, d = x.shape
    n = ts + POOL_HALO
    group = w_ref.shape[-1]

    @pl.when(seq_tile == 0)
    def _():
        ext[0:POOL_HALO, :] = jnp.zeros((POOL_HALO, d), F32)

    ext[POOL_HALO:n, :] = _rms(x, g_ref[...])

    src, dst = ext, buf_a
    for level in range(1, len(POOL_WINDOWS) + 1):
        shift = 1 << (level - 1)
        lo = SUBLANES * level
        c0 = (level - 1) * group
        dst[lo:n, c0:d] = src[lo:n, c0:d] + src[lo - shift:n - shift, c0:d]
        src, dst = dst, (buf_b if dst is buf_a else buf_a)

    pos1 = (seq_tile * ts + lax.broadcasted_iota(jnp.int32, (ts, 1), 0) + 1).astype(F32)
    for g, win in enumerate(POOL_WINDOWS):
        level_buf = buf_a if g % 2 == 0 else buf_b
        cols = slice(g * group, (g + 1) * group)
        inv_cnt = 1.0 / jnp.minimum(pos1, float(win))
        pooled = level_buf[POOL_HALO:n, cols] * inv_cnt - ext[POOL_HALO:n, cols]
        mixed = _dot(pooled.astype(BF16), w_ref[g])
        dst_ref[:, cols] = x[:, cols] + mixed * sc_ref[:, cols]

    ext[0:POOL_HALO, :] = ext[ts:n, :]


def _ffn_kernel(*refs, mixer, per_seq):
    if mixer == "proj":
        x_ref, a_ref, wo_ref, g_ref, wi_ref, wd_ref, o_ref, h_scr, act_scr, x1_scr = refs
        x1_scr[...] = x_ref[...] + _dot(a_ref[...], wo_ref[...])
    else:
        x_ref, pg_ref, pw_ref, psc_ref, g_ref, wi_ref, wd_ref, o_ref, h_scr, act_scr, x1_scr, ext, buf_a, buf_b = refs
        _pool_residual(x_ref[...], pl.program_id(0) % per_seq, pg_ref, pw_ref, psc_ref, ext, buf_a, buf_b, x1_scr)
    h_scr[...] = _rms(x1_scr[...], g_ref[...]).astype(BF16)
    hidden = wd_ref.shape[0]
    for f0 in range(0, hidden, PROJ_TILE):
        gate = _dot(h_scr[...], wi_ref[:, f0:f0 + PROJ_TILE])
        up = _dot(h_scr[...], wi_ref[:, hidden + f0:hidden + f0 + PROJ_TILE])
        act_scr[:, f0:f0 + PROJ_TILE] = (_silu(gate) * up).astype(BF16)
    for c0 in range(0, o_ref.shape[1], PROJ_TILE):
        cols = slice(c0, c0 + PROJ_TILE)
        o_ref[:, cols] = x1_scr[:, cols] + _dot(act_scr[...], wd_ref[:, cols])


def _ffn(x2, gain, w_in, w_out, *, seq, proj=None, pool=None, tm=512):
    t, d = x2.shape
    hidden = w_out.shape[0]
    assert t % tm == 0 and seq % tm == 0 and hidden % PROJ_TILE == 0 and d % PROJ_TILE == 0
    assert (proj is None) != (pool is None)
    row = lambda width: pl.BlockSpec((tm, width), lambda i: (i, 0))
    scratch = [pltpu.VMEM((tm, d), BF16), pltpu.VMEM((tm, hidden), BF16), pltpu.VMEM((tm, d), F32)]
    if proj is not None:
        a2, w_o = proj
        in_specs = [row(d), row(a2.shape[1]), _resident(w_o.shape)]
        args = [x2, a2, w_o]
    else:
        pool_gain, w_group, scale = pool
        assert w_group.shape[0] == len(POOL_WINDOWS) and w_group.shape[0] * w_group.shape[1] == d
        in_specs = [row(d), _resident((1, d)), _resident(w_group.shape), _resident((1, d))]
        args = [x2, pool_gain.reshape(1, d), w_group, scale.reshape(1, d)]
        scratch += [pltpu.VMEM((tm + POOL_HALO, d), F32)] * 3
    in_specs += [_resident((1, d)), _resident(w_in.shape), _resident(w_out.shape)]
    args += [gain.reshape(1, d), w_in, w_out]
    return pl.pallas_call(
        functools.partial(_ffn_kernel, mixer="proj" if proj is not None else "pool", per_seq=seq // tm),
        grid=(t // tm,),
        in_specs=in_specs,
        out_specs=row(d),
        out_shape=jax.ShapeDtypeStruct((t, d), F32),
        scratch_shapes=scratch,
        compiler_params=_params("parallel" if proj is not None else "arbitrary"),
        name="ffn",
    )(*args)


def _sb_proj_kernel(x_ref, g_ref, w_ref, qg_ref, kg_ref, o_ref, h_scr, *, width):
    h_scr[...] = _rms(x_ref[...], g_ref[...]).astype(BF16)
    gains = (qg_ref[...] * (SB_HEAD_DIM ** -0.5 * LOG2E), kg_ref[...], None)
    for tile in range(w_ref.shape[1] // PROJ_TILE):
        cols = slice(tile * PROJ_TILE, (tile + 1) * PROJ_TILE)
        y = _dot(h_scr[...], w_ref[:, cols])
        gain = gains[tile * PROJ_TILE // width]
        if gain is None:
            o_ref[:, cols] = y.astype(o_ref.dtype)
        else:
            for c in range(PROJ_TILE // SB_HEAD_DIM):
                sub = slice(c * SB_HEAD_DIM, (c + 1) * SB_HEAD_DIM)
                o_ref[:, tile * PROJ_TILE + c * SB_HEAD_DIM:tile * PROJ_TILE + (c + 1) * SB_HEAD_DIM] = (
                    _rms(y[:, sub], gain).astype(o_ref.dtype))


def _sb_proj(x2, gain, w_qkv, q_gain, k_gain, *, tm=512):
    t, d = x2.shape
    n = w_qkv.shape[1]
    width = n // 3
    assert t % tm == 0 and width % PROJ_TILE == 0 and PROJ_TILE % SB_HEAD_DIM == 0
    return pl.pallas_call(
        functools.partial(_sb_proj_kernel, width=width),
        grid=(t // tm,),
        in_specs=[
            pl.BlockSpec((tm, d), lambda i: (i, 0)),
            _resident((1, d)),
            _resident((d, n)),
            _resident((1, SB_HEAD_DIM)),
            _resident((1, SB_HEAD_DIM)),
        ],
        out_specs=pl.BlockSpec((tm, n), lambda i: (i, 0)),
        out_shape=jax.ShapeDtypeStruct((t, n), BF16),
        scratch_shapes=[pltpu.VMEM((tm, d), BF16)],
        compiler_params=_params("parallel"),
        name="sb_proj",
    )(x2, gain.reshape(1, d), w_qkv, q_gain.reshape(1, -1), k_gain.reshape(1, -1))


def _sb_attn_kernel(q_ref, k_ref, v_ref, o_ref, *, tq, unroll):
    i = pl.program_id(2)
    tk = tq
    dh = SB_HEAD_DIM
    hp = q_ref.shape[1] // dh
    lanes = [slice(h * dh, (h + 1) * dh) for h in range(hp)]
    q = [q_ref[:, ln] for ln in lanes]
    row = lax.broadcasted_iota(jnp.int32, (tq, tk), 0)
    col = lax.broadcasted_iota(jnp.int32, (tq, tk), 1)
    causal = col < row
    jj = lax.broadcasted_iota(jnp.int32, (tk, tk), 0)
    ss = lax.broadcasted_iota(jnp.int32, (tk, tk), 1)
    suffix = jnp.where(jj > ss, 1.0, 0.0).astype(BF16)

    def scores(h, kb, keep):
        start = pl.multiple_of(kb * tk, tk)
        z = _dot_nt(q[h], k_ref[pl.ds(start, tk), lanes[h]])
        p = jnp.where(z > SOFTPLUS_LINEAR, z, jnp.log(1.0 + jnp.exp2(z)) * LOG2E)
        log_beta = z - p
        if keep is not None:
            p = jnp.where(keep, p, 0.0)
        return p.astype(BF16), log_beta, p[:, :1]

    def blocks(kbs, keeps, acc, run):
        acc, run = list(acc), list(run)
        stage1 = [[scores(h, kb, keep) for kb, keep in zip(kbs, keeps)] for h in range(hp)]
        afters = [[_dot(p_bf, suffix) for p_bf, _, _ in per_head] for per_head in stage1]
        for n, (kb, keep) in enumerate(zip(kbs, keeps)):
            start = pl.multiple_of(kb * tk, tk)
            for h in range(hp):
                _, log_beta, first_col = stage1[h][n]
                after = afters[h][n]
                wts = jnp.exp2(log_beta - (after + run[h]))
                if keep is not None:
                    wts = jnp.where(keep, wts, 0.0)
                acc[h] = acc[h] + _dot(wts.astype(BF16), v_ref[pl.ds(start, tk), lanes[h]])
                run[h] = run[h] + (after[:, :1] + first_col)
        return tuple(acc), tuple(run)

    def live(run):
        return functools.reduce(jnp.minimum, [jnp.min(r) for r in run]) < UNDERFLOW_LOG2

    def head(n_extra):
        def branch():
            acc = tuple(jnp.zeros((tq, dh), F32) for _ in range(hp))
            run = tuple(jnp.zeros((tq, 1), F32) for _ in range(hp))
            kbs = [i] + [i - 1 - u for u in range(n_extra)]
            return blocks(kbs, [causal] + [None] * n_extra, acc, run)
        return branch

    n_groups = (jnp.maximum(i - 1, 0) + unroll - 1) // unroll

    def group(carry):
        t, acc, run, _ = carry
        kbs = [i - 2 - unroll * t - u for u in range(unroll)]
        acc, run = blocks([jnp.maximum(kb, 0) for kb in kbs], [kb >= 0 for kb in kbs], acc, run)
        return t + 1, acc, run, live(run)

    acc, run = lax.switch(jnp.minimum(i, 1), [head(0), head(1)])
    _, acc, run, _ = lax.while_loop(lambda c: (c[0] < n_groups) & c[3], group,
                                    (jnp.int32(0), acc, run, live(run)))
    for h in range(hp):
        o_ref[:, lanes[h]] = acc[h].astype(o_ref.dtype)


def _sb_attn(qkv, *, heads, tq=256, unroll=2, hp=4):
    b, s, _ = qkv.shape
    width = hp * SB_HEAD_DIM
    nhb = heads // hp
    assert s % tq == 0 and heads % hp == 0
    return pl.pallas_call(
        functools.partial(_sb_attn_kernel, tq=tq, unroll=unroll),
        grid=(b, nhb, s // tq),
        in_specs=[
            pl.BlockSpec((None, tq, width), lambda bi, h, i: (bi, i, h)),
            pl.BlockSpec((None, s, width), lambda bi, h, i: (bi, 0, nhb + h)),
            pl.BlockSpec((None, s, width), lambda bi, h, i: (bi, 0, 2 * nhb + h)),
        ],
        out_specs=pl.BlockSpec((None, tq, width), lambda bi, h, i: (bi, i, h)),
        out_shape=jax.ShapeDtypeStruct((b, s, heads * SB_HEAD_DIM), BF16),
        compiler_params=_params("parallel", "parallel", "arbitrary"),
        name="sb_attn",
    )(qkv, qkv, qkv)


def _ret_proj_kernel(x_ref, g_ref, w_ref, cos_ref, sin_ref, o_ref, h_scr, *, dk, heads):
    h_scr[...] = _rms(x_ref[...], g_ref[...]).astype(BF16)
    half = dk // 2
    cos = cos_ref[...]
    sin = sin_ref[...]
    for tile in range(w_ref.shape[1] // dk):
        c0 = tile * dk
        y = _dot(h_scr[...], w_ref[:, c0:c0 + dk])
        if tile < 2 * heads:
            if tile >= heads:
                y = y * (dk ** -0.5)
            x1 = y[:, :half]
            x2 = y[:, half:]
            o_ref[:, c0:c0 + half] = (x1 * cos - x2 * sin).astype(o_ref.dtype)
            o_ref[:, c0 + half:c0 + dk] = (x1 * sin + x2 * cos).astype(o_ref.dtype)
        else:
            o_ref[:, c0:c0 + dk] = y.astype(o_ref.dtype)


def _ret_proj(x2, gain, w_in, cos, sin, *, seq, dk, heads, tm=512):
    t, d = x2.shape
    n = w_in.shape[1]
    assert t % tm == 0 and seq % tm == 0 and n % dk == 0 and dk == PROJ_TILE
    per_seq = seq // tm
    return pl.pallas_call(
        functools.partial(_ret_proj_kernel, dk=dk, heads=heads),
        grid=(t // tm,),
        in_specs=[
            pl.BlockSpec((tm, d), lambda i: (i, 0)),
            _resident((1, d)),
            _resident((d, n)),
            pl.BlockSpec((tm, dk // 2), lambda i: (i % per_seq, 0)),
            pl.BlockSpec((tm, dk // 2), lambda i: (i % per_seq, 0)),
        ],
        out_specs=pl.BlockSpec((tm, n), lambda i: (i, 0)),
        out_shape=jax.ShapeDtypeStruct((t, n), BF16),
        scratch_shapes=[pltpu.VMEM((tm, d), BF16)],
        compiler_params=_params("parallel"),
        name="ret_proj",
    )(x2, gain.reshape(1, d), w_in, cos, sin)


def _ret_kernel(lg_ref, q_ref, k_ref, v_ref, g_ref, gn_ref, o_ref, state, *, rows, chunk, heads):
    c = pl.program_id(1)
    dk = q_ref.shape[1] // heads
    dv = v_ref.shape[1] // heads

    @pl.when(c == 0)
    def _():
        state[...] = jnp.zeros_like(state)

    ri = lax.broadcasted_iota(jnp.int32, (chunk, chunk), 0)
    ci = lax.broadcasted_iota(jnp.int32, (chunk, chunk), 1)
    diff = (ri - ci).astype(F32)
    idx = lax.broadcasted_iota(jnp.int32, (chunk, 1), 0).astype(F32)
    lg = [lg_ref[h][:, :1] for h in range(heads)]
    intra = [jnp.where(diff >= 0, jnp.exp(jnp.maximum(diff, 0.0) * l), 0.0) for l in lg]
    q_dec = [jnp.exp((idx + 1.0) * l) for l in lg]
    k_dec = [jnp.exp((chunk - 1.0 - idx) * l) for l in lg]
    chunk_dec = [jnp.exp(chunk * l) for l in lg]

    st = [state[h] for h in range(heads)]
    for n in range(rows // chunk):
        r = slice(n * chunk, (n + 1) * chunk)
        q = [q_ref[r, h * dk:(h + 1) * dk] for h in range(heads)]
        k = [k_ref[r, h * dk:(h + 1) * dk] for h in range(heads)]
        v = [v_ref[r, h * dv:(h + 1) * dv] for h in range(heads)]
        scores = [(_dot_nt(a, b) * m).astype(BF16) for a, b, m in zip(q, k, intra)]
        kd = [(b.astype(F32) * d).astype(BF16) for b, d in zip(k, k_dec)]
        inter = [_dot(a, s.astype(BF16)) for a, s in zip(q, st)]
        kv = [_dot_tn(a, b) for a, b in zip(kd, v)]
        o = [_dot(sc, b) + d * it for sc, b, d, it in zip(scores, v, q_dec, inter)]
        st = [s * d + u for s, d, u in zip(st, chunk_dec, kv)]
        for h in range(heads):
            gate = g_ref[r, h * dv:(h + 1) * dv].astype(F32)
            o_ref[r, h * dv:(h + 1) * dv] = (_rms(o[h], gn_ref[...]) * _silu(gate)).astype(o_ref.dtype)
    for h in range(heads):
        state[h] = st[h]


def _ret_core(proj, lg_rows, gn_gain, *, heads, dk, dv, rows=512):
    b, s, _ = proj.shape
    assert s % rows == 0 and rows % RET_CHUNK == 0 and dv == 2 * dk
    return pl.pallas_call(
        functools.partial(_ret_kernel, rows=rows, chunk=RET_CHUNK, heads=heads),
        grid=(b, s // rows),
        in_specs=[
            _resident((heads, 1, LANES)),
            pl.BlockSpec((None, rows, heads * dk), lambda bi, c: (bi, c, 0)),
            pl.BlockSpec((None, rows, heads * dk), lambda bi, c: (bi, c, 1)),
            pl.BlockSpec((None, rows, heads * dv), lambda bi, c: (bi, c, 1)),
            pl.BlockSpec((None, rows, heads * dv), lambda bi, c: (bi, c, 2)),
            _resident((1, dv)),
        ],
        out_specs=pl.BlockSpec((None, rows, heads * dv), lambda bi, c: (bi, c, 0)),
        out_shape=jax.ShapeDtypeStruct((b, s, heads * dv), BF16),
        scratch_shapes=[pltpu.VMEM((heads, dk, dv), F32)],
        compiler_params=_params("parallel", "arbitrary"),
        name="ret_core",
    )(lg_rows, proj, proj, proj, proj, gn_gain.reshape(1, dv))


def _gdn_proj_kernel(x_ref, g_ref, w_ref, wab_ref, cw_ref, alog_ref, dtb_ref, o_ref, bg_ref,
                     h_scr, halo, *, per_seq, heads, dk):
    i = pl.program_id(0)
    tm = x_ref.shape[0]
    n_conv = cw_ref.shape[1]

    @pl.when(i == 0)
    def _():
        halo[...] = jnp.zeros_like(halo)

    h_scr[...] = _rms(x_ref[...], g_ref[...]).astype(BF16)
    first = (i % per_seq) == 0
    row8 = lax.broadcasted_iota(jnp.int32, (SUBLANES, PROJ_TILE), 0)
    conv_tiles = list(range(n_conv // PROJ_TILE))
    plain_tiles = list(range(n_conv // PROJ_TILE, w_ref.shape[1] // PROJ_TILE))
    per_plain = -(-len(conv_tiles) // max(len(plain_tiles), 1))
    order = []
    for n, tile in enumerate(conv_tiles):
        order.append(tile)
        if (n + 1) % per_plain == 0 and plain_tiles:
            order.append(plain_tiles.pop(0))
    order += plain_tiles
    for tile in order:
        c0 = tile * PROJ_TILE
        cols = slice(c0, c0 + PROJ_TILE)
        y = _dot(h_scr[...], w_ref[:, cols])
        if c0 >= n_conv:
            o_ref[:, cols] = y.astype(o_ref.dtype)
            continue
        prev = jnp.where(first, 0.0, halo[:, cols])
        halo[:, cols] = y[tm - SUBLANES:tm, :]
        for r0 in range(0, tm, CONV_ROWS):
            rows = slice(r0, r0 + CONV_ROWS)
            yc = y[rows]
            conv = cw_ref[GDN_CONV - 1:GDN_CONV, cols] * yc
            for back in range(1, GDN_CONV):
                shifted = pltpu.roll(yc, back, axis=0)
                top = jnp.where(row8 < back, pltpu.roll(prev, back, axis=0), shifted[:SUBLANES])
                shifted = jnp.concatenate([top, shifted[SUBLANES:]], axis=0)
                conv = conv + cw_ref[GDN_CONV - 1 - back:GDN_CONV - back, cols] * shifted
            prev = yc[CONV_ROWS - SUBLANES:]
            act = _silu(conv)
            if c0 >= 2 * heads * dk:
                o_ref[rows, cols] = act.astype(o_ref.dtype)
                continue
            scale = dk ** -0.5 if c0 < heads * dk else 1.0
            for c in range(PROJ_TILE // dk):
                a = act[:, c * dk:(c + 1) * dk]
                nrm = lax.rsqrt(jnp.sum(a * a, axis=-1, keepdims=True) + EPS) * scale
                o_ref[rows, c0 + c * dk:c0 + (c + 1) * dk] = (a * nrm).astype(o_ref.dtype)

    y = _dot(h_scr[...], wab_ref[...])
    a = y + dtb_ref[...]
    softplus = jnp.maximum(a, 0.0) + jnp.log(1.0 + jnp.exp(-jnp.abs(a)))
    log_decay = -jnp.exp(alog_ref[...]) * softplus
    beta = jax.nn.sigmoid(y)
    lane = lax.broadcasted_iota(jnp.int32, y.shape, 1)
    bg_ref[...] = jnp.where(lane < heads, log_decay, beta)


def _gdn_proj(x2, gain, w_main, w_ab, conv_w, a_log_row, dt_bias_row, *, seq, heads, dk, tm=512):
    t, d = x2.shape
    n = w_main.shape[1]
    n_conv = conv_w.shape[1]
    assert t % tm == 0 and seq % tm == 0 and n % PROJ_TILE == 0 and n_conv % PROJ_TILE == 0
    assert PROJ_TILE % dk == 0 and (heads * dk) % PROJ_TILE == 0
    per_seq = seq // tm
    return pl.pallas_call(
        functools.partial(_gdn_proj_kernel, per_seq=per_seq, heads=heads, dk=dk),
        grid=(t // tm,),
        in_specs=[
            pl.BlockSpec((tm, d), lambda i: (i, 0)),
            _resident((1, d)),
            _resident((d, n)),
            _resident((d, LANES)),
            _resident((GDN_CONV, n_conv)),
            _resident((1, LANES)),
            _resident((1, LANES)),
        ],
        out_specs=[pl.BlockSpec((tm, n), lambda i: (i, 0)), pl.BlockSpec((tm, LANES), lambda i: (i, 0))],
        out_shape=[jax.ShapeDtypeStruct((t, n), BF16), jax.ShapeDtypeStruct((t, LANES), F32)],
        scratch_shapes=[
            pltpu.VMEM((tm, d), BF16),
            pltpu.VMEM((SUBLANES, n_conv), F32),
        ],
        compiler_params=_params("arbitrary"),
        name="gdn_proj",
    )(x2, gain.reshape(1, d), w_main, w_ab, conv_w, a_log_row, dt_bias_row)


def _gdn_kernel(q_ref, k_ref, v_ref, z_ref, bg_ref, gain_ref, o_ref, state, *, rows, chunk, heads, hp):
    hb = pl.program_id(1)
    c = pl.program_id(2)
    C = chunk
    nc = rows // C
    dk = q_ref.shape[1] // hp
    dv = v_ref.shape[1] // hp

    @pl.when(c == 0)
    def _():
        state[...] = jnp.zeros_like(state)

    lane = lax.broadcasted_iota(jnp.int32, (C, LANES), 1)
    ri = lax.broadcasted_iota(jnp.int32, (C, C), 0)
    ci = lax.broadcasted_iota(jnp.int32, (C, C), 1)
    tril = ci <= ri
    strict = ci < ri
    eye = jnp.where(ci == ri, 1.0, 0.0)
    ai = lax.broadcasted_iota(jnp.int32, (2 * C, C), 0)
    am = lax.broadcasted_iota(jnp.int32, (2 * C, C), 1)
    sum_sel = jnp.where(((ai < C) & (am <= ai)) | ((ai >= C) & (am > ai - C)), 1.0, 0.0).astype(BF16)
    sum_sel2 = jnp.concatenate([sum_sel, sum_sel], axis=1)
    bm = lax.broadcasted_iota(jnp.int32, (C, 2 * C), 0)
    bj = lax.broadcasted_iota(jnp.int32, (C, 2 * C), 1)
    col_sel = jnp.where((bm > bj) | (bj >= C), 1.0, 0.0)

    items = [(n, h) for n in range(nc) for h in range(hp)]
    r_of = lambda n: slice(n * C, (n + 1) * C)
    c_of = lambda h, w: slice(h * w, (h + 1) * w)
    bgs = [bg_ref[r_of(n), :] for n in range(nc)]
    pick = lambda n, ln: jnp.sum(jnp.where(lane == ln, bgs[n], 0.0), axis=-1, keepdims=True)
    g_col = [pick(n, hb * hp + h) for n, h in items]
    b_col = [pick(n, hb * hp + h + heads) for n, h in items]

    gmat = [g * col_sel for g in g_col]
    g_hi = [m.astype(BF16) for m in gmat]
    g_split = [jnp.concatenate([hi, (m - hi.astype(F32)).astype(BF16)], axis=0) for m, hi in zip(gmat, g_hi)]
    sums = [_dot(sum_sel2, m) for m in g_split]
    decay = [jnp.where(tril, jnp.exp(s[:C, :C]), 0.0) for s in sums]
    g_cum = [s[:C, C:C + 1] for s in sums]
    exp_g = [jnp.exp(g) for g in g_cum]
    exp_rem = [jnp.exp(s[C:, C:C + 1]) for s in sums]
    exp_last = [e[C - 1:C, :] for e in exp_g]

    q = [q_ref[r_of(n), c_of(h, dk)] for n, h in items]
    k = [k_ref[r_of(n), c_of(h, dk)] for n, h in items]
    kf = [x.astype(F32) for x in k]
    k_beta = [x * b for x, b in zip(kf, b_col)]
    kq = [_dot_nt(jnp.concatenate([kb.astype(BF16), a], axis=0), x) for kb, a, x in zip(k_beta, q, k)]
    n_mat = [jnp.where(strict, m[:C] * d, 0.0) for m, d in zip(kq, decay)]
    attn = [jnp.where(tril, m[C:] * d, 0.0).astype(BF16) for m, d in zip(kq, decay)]
    inv = [eye - m for m in n_mat]
    n_bf = [m.astype(BF16) for m in n_mat]
    power = [_dot(m, m) for m in n_bf]
    for _ in range(4):
        p_bf = [p.astype(BF16) for p in power]
        both = [_dot(jnp.concatenate([x.astype(BF16), p], axis=0), p) for x, p in zip(inv, p_bf)]
        inv = [x + m[:C] for x, m in zip(inv, both)]
        power = [m[C:] for m in both]
    inv = [x + _dot(x.astype(BF16), p.astype(BF16)) for x, p in zip(inv, power)]

    rhs = [jnp.concatenate([v_ref[r_of(n), c_of(h, dv)].astype(F32) * b, kb * eg], axis=1).astype(BF16)
           for (n, h), b, kb, eg in zip(items, b_col, k_beta, exp_g)]
    uw = [_dot(x.astype(BF16), r) for x, r in zip(inv, rhs)]
    kd = [(x * e).astype(BF16) for x, e in zip(kf, exp_rem)]

    st = [state[h] for h in range(hp)]
    for n in range(nc):
        idx = [n * hp + h for h in range(hp)]
        wq = [jnp.concatenate([uw[i][:, dv:].astype(BF16), q[i]], axis=0) for i in idx]
        ws = [_dot(a, s.astype(BF16)) for a, s in zip(wq, st)]
        v_new = [(uw[i][:, :dv] - w[:C]).astype(BF16) for i, w in zip(idx, ws)]
        o = [exp_g[i] * w[C:] + _dot(attn[i], vn) for i, w, vn in zip(idx, ws, v_new)]
        st = [s * exp_last[i] + _dot_tn(kd[i], vn) for i, s, vn in zip(idx, st, v_new)]
        for h in range(hp):
            gate = z_ref[r_of(n), c_of(h, dv)].astype(F32)
            o_ref[r_of(n), c_of(h, dv)] = (_rms(o[h], gain_ref[...]) * _silu(gate)).astype(o_ref.dtype)
    for h in range(hp):
        state[h] = st[h]


def _gdn_core(proj, bg, norm_gain, *, heads, dk, dv, rows=512, hp=8):
    b, s, _ = proj.shape
    assert s % rows == 0 and rows % GDN_CHUNK == 0 and dv == 2 * dk and GDN_CHUNK == 64
    assert heads % hp == 0
    nhb = heads // hp
    return pl.pallas_call(
        functools.partial(_gdn_kernel, rows=rows, chunk=GDN_CHUNK, heads=heads, hp=hp),
        grid=(b, nhb, s // rows),
        in_specs=[
            pl.BlockSpec((None, rows, hp * dk), lambda bi, h, c: (bi, c, h)),
            pl.BlockSpec((None, rows, hp * dk), lambda bi, h, c: (bi, c, nhb + h)),
            pl.BlockSpec((None, rows, hp * dv), lambda bi, h, c: (bi, c, nhb + h)),
            pl.BlockSpec((None, rows, hp * dv), lambda bi, h, c: (bi, c, 2 * nhb + h)),
            pl.BlockSpec((None, rows, LANES), lambda bi, h, c: (bi, c, 0)),
            pl.BlockSpec((1, dv), lambda bi, h, c: (0, 0)),
        ],
        out_specs=pl.BlockSpec((None, rows, hp * dv), lambda bi, h, c: (bi, c, h)),
        out_shape=jax.ShapeDtypeStruct((b, s, heads * dv), BF16),
        scratch_shapes=[pltpu.VMEM((hp, dk, dv), F32)],
        compiler_params=_params("parallel", "parallel", "arbitrary"),
        name="gdn_core",
    )(proj, proj, proj, proj, bg, norm_gain.reshape(1, dv))


def _sb_mixer(x, gain, w_qkv, q_gain, k_gain):
    b, s, d = x.shape
    heads = w_qkv.shape[1] // (3 * SB_HEAD_DIM)
    qkv = _sb_proj(x.reshape(b * s, d), gain, w_qkv.astype(BF16), q_gain, k_gain)
    return _sb_attn(qkv.reshape(b, s, -1), heads=heads)


def _ret_mixer(x, gain, w_in, gn_gain):
    b, s, d = x.shape
    heads = RET_HEADS
    dv = gn_gain.shape[0]
    dk = dv // 2
    half = dk // 2
    inv = ROPE_BASE ** (-jnp.arange(half, dtype=F32) / half)
    ang = jnp.arange(s, dtype=F32)[:, None] * inv[None, :]
    log_gamma = jnp.log1p(-jnp.exp2(-5.0 - jnp.arange(heads, dtype=F32)))
    lg_rows = jnp.broadcast_to(log_gamma[:, None, None], (heads, 1, LANES))
    proj = _ret_proj(x.reshape(b * s, d), gain, w_in.astype(BF16), jnp.cos(ang), jnp.sin(ang),
                     seq=s, dk=dk, heads=heads)
    return _ret_core(proj.reshape(b, s, -1), lg_rows, gn_gain, heads=heads, dk=dk, dv=dv)


def _gdn_mixer(x, gain, w_in, conv_w, a_log, dt_bias, norm_gain):
    b, s, d = x.shape
    heads = a_log.shape[0]
    dk = GDN_K_DIM
    dv = norm_gain.shape[0]
    n_main = 2 * heads * dk + 2 * heads * dv
    w_ab = jnp.zeros((d, LANES), BF16).at[:, :2 * heads].set(w_in[:, n_main:].astype(BF16))
    pad = lambda p: jnp.zeros((1, LANES), F32).at[0, :heads].set(p)
    proj, bg = _gdn_proj(x.reshape(b * s, d), gain, w_in[:, :n_main].astype(BF16), w_ab, conv_w,
                         pad(a_log), pad(dt_bias), seq=s, heads=heads, dk=dk)
    return _gdn_core(proj.reshape(b, s, -1), bg.reshape(b, s, LANES), norm_gain, heads=heads, dk=dk, dv=dv)


def _ffn_layer(x, gain, w_in, w_out, *, o=None, w_o=None, pool=None):
    b, s, d = x.shape
    proj = None if o is None else (o.reshape(b * s, -1), w_o.astype(BF16))
    out = _ffn(x.reshape(b * s, d), gain, w_in.astype(BF16), w_out.astype(BF16), seq=s, proj=proj, pool=pool)
    return out.reshape(b, s, d)


def kernel(x, norm_mix, norm_ffn, ffn_w_in, ffn_w_out, pool_w, pool_scale, sb_w_qkv, sb_q_gain, sb_k_gain, sb_w_o, ret_w_in, ret_gn_gain, ret_w_o, gdn_w_in, gdn_conv_w, gdn_a_log, gdn_dt_bias, gdn_norm_gain, gdn_w_o):
    depth = norm_mix.shape[0]
    for layer in range(depth):
        mixer, occ = layer % 4, layer // 4
        ffn = functools.partial(_ffn_layer, x, norm_ffn[layer], ffn_w_in[layer], ffn_w_out[layer])
        if mixer == 0:
            x = ffn(pool=(norm_mix[layer], pool_w[occ].astype(BF16), pool_scale[occ]))
        elif mixer == 1:
            x = ffn(o=_sb_mixer(x, norm_mix[layer], sb_w_qkv[occ], sb_q_gain[occ], sb_k_gain[occ]), w_o=sb_w_o[occ])
        elif mixer == 2:
            x = ffn(o=_ret_mixer(x, norm_mix[layer], ret_w_in[occ], ret_gn_gain[occ]), w_o=ret_w_o[occ])
        else:
            x = ffn(o=_gdn_mixer(x, norm_mix[layer], gdn_w_in[occ], gdn_conv_w[occ], gdn_a_log[occ],
                                 gdn_dt_bias[occ], gdn_norm_gain[occ]), w_o=gdn_w_o[occ])
    return x
```

```python
import functools

import jax
import jax.numpy as jnp
from jax import lax
from jax.experimental import pallas as pl
from jax.experimental.pallas import tpu as pltpu

F32 = jnp.float32
BF16 = jnp.bfloat16

EPS = 1e-6
LOG2E = 1.4426950408889634
SOFTPLUS_LINEAR = 64.0
UNDERFLOW_LOG2 = 160.0
ROPE_BASE = 10000.0
LANES = 128
SUBLANES = 8
VMEM_LIMIT = 56 * 1024 * 1024
PROJ_TILE = 256

POOL_WINDOWS = (2, 4, 8, 16)
POOL_HALO = 32
SB_HEAD_DIM = 128
RET_HEADS = 4
RET_CHUNK = 256
GDN_K_DIM = 128
GDN_CONV = 4
GDN_CHUNK = 64
CONV_ROWS = 256


def _params(*sem):
    return pltpu.CompilerParams(dimension_semantics=sem, vmem_limit_bytes=VMEM_LIMIT)


def _resident(shape):
    return pl.BlockSpec(shape, lambda *_: (0,) * len(shape), pipeline_mode=pl.Buffered(1))


def _rms(xf, gain_row):
    ms = jnp.mean(xf * xf, axis=-1, keepdims=True)
    return xf * lax.rsqrt(ms + EPS) * gain_row


def _silu(x):
    h = 0.5 * x
    return h + h * jnp.tanh(h)


def _dot(a, b):
    return jnp.dot(a, b, preferred_element_type=F32)


def _dot_nt(a, b):
    return lax.dot_general(a, b, (((1,), (1,)), ((), ())), preferred_element_type=F32)


def _dot_tn(a, b):
    return lax.dot_general(a, b, (((0,), (0,)), ((), ())), preferred_element_type=F32)


def _pool_residual(x, seq_tile, g_ref, w_ref, sc_ref, ext, buf_a, buf_b, dst_ref):
    ts, d = x.shape
    n = ts + POOL_HALO
    group = w_ref.shape[-1]

    @pl.when(seq_tile == 0)
    def _():
        ext[0:POOL_HALO, :] = jnp.zeros((POOL_HALO, d), F32)

    ext[POOL_HALO:n, :] = _rms(x, g_ref[...])

    src, dst = ext, buf_a
    for level in range(1, len(POOL_WINDOWS) + 1):
        shift = 1 << (level - 1)
        lo = SUBLANES * level
        c0 = (level - 1) * group
        dst[lo:n, c0:d] = src[lo:n, c0:d] + src[lo - shift:n - shift, c0:d]
        src, dst = dst, (buf_b if dst is buf_a else buf_a)

    pos1 = (seq_tile * ts + lax.broadcasted_iota(jnp.int32, (ts, 1), 0) + 1).astype(F32)
    for g, win in enumerate(POOL_WINDOWS):
        level_buf = buf_a if g % 2 == 0 else buf_b
        cols = slice(g * group, (g + 1) * group)
        inv_cnt = 1.0 / jnp.minimum(pos1, float(win))
        pooled = level_buf[POOL_HALO:n, cols] * inv_cnt - ext[POOL_HALO:n, cols]
        mixed = _dot(pooled.astype(BF16), w_ref[g])
        dst_ref[:, cols] = x[:, cols] + mixed * sc_ref[:, cols]

    ext[0:POOL_HALO, :] = ext[ts:n, :]


def _ffn_kernel(*refs, mixer, per_seq):
    if mixer == "proj":
        x_ref, a_ref, wo_ref, g_ref, wi_ref, wd_ref, o_ref, h_scr, act_scr, x1_scr = refs
        x1_scr[...] = x_ref[...] + _dot(a_ref[...], wo_ref[...])
    else:
        x_ref, pg_ref, pw_ref, psc_ref, g_ref, wi_ref, wd_ref, o_ref, h_scr, act_scr, x1_scr, ext, buf_a, buf_b = refs
        _pool_residual(x_ref[...], pl.program_id(0) % per_seq, pg_ref, pw_ref, psc_ref, ext, buf_a, buf_b, x1_scr)
    h_scr[...] = _rms(x1_scr[...], g_ref[...]).astype(BF16)
    hidden = wd_ref.shape[0]
    for f0 in range(0, hidden, PROJ_TILE):
        gate = _dot(h_scr[...], wi_ref[:, f0:f0 + PROJ_TILE])
        up = _dot(h_scr[...], wi_ref[:, hidden + f0:hidden + f0 + PROJ_TILE])
        act_scr[:, f0:f0 + PROJ_TILE] = (_silu(gate) * up).astype(BF16)
    for c0 in range(0, o_ref.shape[1], PROJ_TILE):
        cols = slice(c0, c0 + PROJ_TILE)
        o_ref[:, cols] = x1_scr[:, cols] + _dot(act_scr[...], wd_ref[:, cols])


def _ffn(x2, gain, w_in, w_out, *, seq, proj=None, pool=None, tm=512):
    t, d = x2.shape
    hidden = w_out.shape[0]
    assert t % tm == 0 and seq % tm == 0 and hidden % PROJ_TILE == 0 and d % PROJ_TILE == 0
    assert (proj is None) != (pool is None)
    row = lambda width: pl.BlockSpec((tm, width), lambda i: (i, 0))
    scratch = [pltpu.VMEM((tm, d), BF16), pltpu.VMEM((tm, hidden), BF16), pltpu.VMEM((tm, d), F32)]
    if proj is not None:
        a2, w_o = proj
        in_specs = [row(d), row(a2.shape[1]), _resident(w_o.shape)]
        args = [x2, a2, w_o]
    else:
        pool_gain, w_group, scale = pool
        assert w_group.shape[0] == len(POOL_WINDOWS) and w_group.shape[0] * w_group.shape[1] == d
        in_specs = [row(d), _resident((1, d)), _resident(w_group.shape), _resident((1, d))]
        args = [x2, pool_gain.reshape(1, d), w_group, scale.reshape(1, d)]
        scratch += [pltpu.VMEM((tm + POOL_HALO, d), F32)] * 3
    in_specs += [_resident((1, d)), _resident(w_in.shape), _resident(w_out.shape)]
    args += [gain.reshape(1, d), w_in, w_out]
    return pl.pallas_call(
        functools.partial(_ffn_kernel, mixer="proj" if proj is not None else "pool", per_seq=seq // tm),
        grid=(t // tm,),
        in_specs=in_specs,
        out_specs=row(d),
        out_shape=jax.ShapeDtypeStruct((t, d), F32),
        scratch_shapes=scratch,
        compiler_params=_params("parallel" if proj is not None else "arbitrary"),
        name="ffn",
    )(*args)


def _sb_proj_kernel(x_ref, g_ref, w_ref, qg_ref, kg_ref, o_ref, h_scr, *, width):
    h_scr[...] = _rms(x_ref[...], g_ref[...]).astype(BF16)
    gains = (qg_ref[...] * (SB_HEAD_DIM ** -0.5 * LOG2E), kg_ref[...], None)
    for tile in range(w_ref.shape[1] // PROJ_TILE):
        cols = slice(tile * PROJ_TILE, (tile + 1) * PROJ_TILE)
        y = _dot(h_scr[...], w_ref[:, cols])
        gain = gains[tile * PROJ_TILE // width]
        if gain is None:
            o_ref[:, cols] = y.astype(o_ref.dtype)
        else:
            for c in range(PROJ_TILE // SB_HEAD_DIM):
                sub = slice(c * SB_HEAD_DIM, (c + 1) * SB_HEAD_DIM)
                o_ref[:, tile * PROJ_TILE + c * SB_HEAD_DIM:tile * PROJ_TILE + (c + 1) * SB_HEAD_DIM] = (
                    _rms(y[:, sub], gain).astype(o_ref.dtype))


def _sb_proj(x2, gain, w_qkv, q_gain, k_gain, *, tm=512):
    t, d = x2.shape
    n = w_qkv.shape[1]
    width = n // 3
    assert t % tm == 0 and width % PROJ_TILE == 0 and PROJ_TILE % SB_HEAD_DIM == 0
    return pl.pallas_call(
        functools.partial(_sb_proj_kernel, width=width),
        grid=(t // tm,),
        in_specs=[
            pl.BlockSpec((tm, d), lambda i: (i, 0)),
            _resident((1, d)),
            _resident((d, n)),
            _resident((1, SB_HEAD_DIM)),
            _resident((1, SB_HEAD_DIM)),
        ],
        out_specs=pl.BlockSpec((tm, n), lambda i: (i, 0)),
        out_shape=jax.ShapeDtypeStruct((t, n), BF16),
        scratch_shapes=[pltpu.VMEM((tm, d), BF16)],
        compiler_params=_params("parallel"),
        name="sb_proj",
    )(x2, gain.reshape(1, d), w_qkv, q_gain.reshape(1, -1), k_gain.reshape(1, -1))


def _sb_attn_kernel(q_ref, k_ref, v_ref, o_ref, *, tq, unroll):
    i = pl.program_id(2)
    tk = tq
    dh = SB_HEAD_DIM
    hp = q_ref.shape[1] // dh
    lanes = [slice(h * dh, (h + 1) * dh) for h in range(hp)]
    q = [q_ref[:, ln] for ln in lanes]
    row = lax.broadcasted_iota(jnp.int32, (tq, tk), 0)
    col = lax.broadcasted_iota(jnp.int32, (tq, tk), 1)
    causal = col < row
    jj = lax.broadcasted_iota(jnp.int32, (tk, tk), 0)
    ss = lax.broadcasted_iota(jnp.int32, (tk, tk), 1)
    suffix = jnp.where(jj > ss, 1.0, 0.0).astype(BF16)

    def scores(h, kb, keep):
        start = pl.multiple_of(kb * tk, tk)
        z = _dot_nt(q[h], k_ref[pl.ds(start, tk), lanes[h]])
        p = jnp.where(z > SOFTPLUS_LINEAR, z, jnp.log(1.0 + jnp.exp2(z)) * LOG2E)
        log_beta = z - p
        if keep is not None:
            p = jnp.where(keep, p, 0.0)
        return p.astype(BF16), log_beta, p[:, :1]

    def blocks(kbs, keeps, acc, run):
        acc, run = list(acc), list(run)
        stage1 = [[scores(h, kb, keep) for kb, keep in zip(kbs, keeps)] for h in range(hp)]
        afters = [[_dot(p_bf, suffix) for p_bf, _, _ in per_head] for per_head in stage1]
        for n, (kb, keep) in enumerate(zip(kbs, keeps)):
            start = pl.multiple_of(kb * tk, tk)
            for h in range(hp):
                _, log_beta, first_col = stage1[h][n]
                after = afters[h][n]
                wts = jnp.exp2(log_beta - (after + run[h]))
                if keep is not None:
                    wts = jnp.where(keep, wts, 0.0)
                acc[h] = acc[h] + _dot(wts.astype(BF16), v_ref[pl.ds(start, tk), lanes[h]])
                run[h] = run[h] + (after[:, :1] + first_col)
        return tuple(acc), tuple(run)

    def live(run):
        return functools.reduce(jnp.minimum, [jnp.min(r) for r in run]) < UNDERFLOW_LOG2

    def head(n_extra):
        def branch():
            acc = tuple(jnp.zeros((tq, dh), F32) for _ in range(hp))
            run = tuple(jnp.zeros((tq, 1), F32) for _ in range(hp))
            kbs = [i] + [i - 1 - u for u in range(n_extra)]
            return blocks(kbs, [causal] + [None] * n_extra, acc, run)
        return branch

    n_groups = (jnp.maximum(i - 1, 0) + unroll - 1) // unroll

    def group(carry):
        t, acc, run, _ = carry
        kbs = [i - 2 - unroll * t - u for u in range(unroll)]
        acc, run = blocks([jnp.maximum(kb, 0) for kb in kbs], [kb >= 0 for kb in kbs], acc, run)
        return t + 1, acc, run, live(run)

    acc, run = lax.switch(jnp.minimum(i, 1), [head(0), head(1)])
    _, acc, run, _ = lax.while_loop(lambda c: (c[0] < n_groups) & c[3], group,
                                    (jnp.int32(0), acc, run, live(run)))
    for h in range(hp):
        o_ref[:, lanes[h]] = acc[h].astype(o_ref.dtype)


def _sb_attn(qkv, *, heads, tq=256, unroll=2, hp=4):
    b, s, _ = qkv.shape
    width = hp * SB_HEAD_DIM
    nhb = heads // hp
    assert s % tq == 0 and heads % hp == 0
    return pl.pallas_call(
        functools.partial(_sb_attn_kernel, tq=tq, unroll=unroll),
        grid=(b, nhb, s // tq),
        in_specs=[
            pl.BlockSpec((None, tq, width), lambda bi, h, i: (bi, i, h)),
            pl.BlockSpec((None, s, width), lambda bi, h, i: (bi, 0, nhb + h)),
            pl.BlockSpec((None, s, width), lambda bi, h, i: (bi, 0, 2 * nhb + h)),
        ],
        out_specs=pl.BlockSpec((None, tq, width), lambda bi, h, i: (bi, i, h)),
        out_shape=jax.ShapeDtypeStruct((b, s, heads * SB_HEAD_DIM), BF16),
        compiler_params=_params("parallel", "parallel", "arbitrary"),
        name="sb_attn",
    )(qkv, qkv, qkv)


def _ret_proj_kernel(x_ref, g_ref, w_ref, cos_ref, sin_ref, o_ref, h_scr, *, dk, heads):
    h_scr[...] = _rms(x_ref[...], g_ref[...]).astype(BF16)
    half = dk // 2
    cos = cos_ref[...]
    sin = sin_ref[...]
    for tile in range(w_ref.shape[1] // dk):
        c0 = tile * dk
        y = _dot(h_scr[...], w_ref[:, c0:c0 + dk])
        if tile < 2 * heads:
            if tile >= heads:
                y = y * (dk ** -0.5)
            x1 = y[:, :half]
            x2 = y[:, half:]
            o_ref[:, c0:c0 + half] = (x1 * cos - x2 * sin).astype(o_ref.dtype)
            o_ref[:, c0 + half:c0 + dk] = (x1 * sin + x2 * cos).astype(o_ref.dtype)
        else:
            o_ref[:, c0:c0 + dk] = y.astype(o_ref.dtype)


def _ret_proj(x2, gain, w_in, cos, sin, *, seq, dk, heads, tm=512):
    t, d = x2.shape
    n = w_in.shape[1]
    assert t % tm == 0 and seq % tm == 0 and n % dk == 0 and dk == PROJ_TILE
    per_seq = seq // tm
    return pl.pallas_call(
        functools.partial(_ret_proj_kernel, dk=dk, heads=heads),
        grid=(t // tm,),
        in_specs=[
            pl.BlockSpec((tm, d), lambda i: (i, 0)),
            _resident((1, d)),
            _resident((d, n)),
            pl.BlockSpec((tm, dk // 2), lambda i: (i % per_seq, 0)),
            pl.BlockSpec((tm, dk // 2), lambda i: (i % per_seq, 0)),
        ],
        out_specs=pl.BlockSpec((tm, n), lambda i: (i, 0)),
        out_shape=jax.ShapeDtypeStruct((t, n), BF16),
        scratch_shapes=[pltpu.VMEM((tm, d), BF16)],
        compiler_params=_params("parallel"),
        name="ret_proj",
    )(x2, gain.reshape(1, d), w_in, cos, sin)


def _ret_kernel(lg_ref, q_ref, k_ref, v_ref, g_ref, gn_ref, o_ref, state, *, rows, chunk, heads):
    c = pl.program_id(1)
    dk = q_ref.shape[1] // heads
    dv = v_ref.shape[1] // heads

    @pl.when(c == 0)
    def _():
        state[...] = jnp.zeros_like(state)

    ri = lax.broadcasted_iota(jnp.int32, (chunk, chunk), 0)
    ci = lax.broadcasted_iota(jnp.int32, (chunk, chunk), 1)
    diff = (ri - ci).astype(F32)
    idx = lax.broadcasted_iota(jnp.int32, (chunk, 1), 0).astype(F32)
    lg = [lg_ref[h][:, :1] for h in range(heads)]
    intra = [jnp.where(diff >= 0, jnp.exp(jnp.maximum(diff, 0.0) * l), 0.0) for l in lg]
    q_dec = [jnp.exp((idx + 1.0) * l) for l in lg]
    k_dec = [jnp.exp((chunk - 1.0 - idx) * l) for l in lg]
    chunk_dec = [jnp.exp(chunk * l) for l in lg]

    st = [state[h] for h in range(heads)]
    for n in range(rows // chunk):
        r = slice(n * chunk, (n + 1) * chunk)
        q = [q_ref[r, h * dk:(h + 1) * dk] for h in range(heads)]
        k = [k_ref[r, h * dk:(h + 1) * dk] for h in range(heads)]
        v = [v_ref[r, h * dv:(h + 1) * dv] for h in range(heads)]
        scores = [(_dot_nt(a, b) * m).astype(BF16) for a, b, m in zip(q, k, intra)]
        kd = [(b.astype(F32) * d).astype(BF16) for b, d in zip(k, k_dec)]
        inter = [_dot(a, s.astype(BF16)) for a, s in zip(q, st)]
        kv = [_dot_tn(a, b) for a, b in zip(kd, v)]
        o = [_dot(sc, b) + d * it for sc, b, d, it in zip(scores, v, q_dec, inter)]
        st = [s * d + u for s, d, u in zip(st, chunk_dec, kv)]
        for h in range(heads):
            gate = g_ref[r, h * dv:(h + 1) * dv].astype(F32)
            o_ref[r, h * dv:(h + 1) * dv] = (_rms(o[h], gn_ref[...]) * _silu(gate)).astype(o_ref.dtype)
    for h in range(heads):
        state[h] = st[h]


def _ret_core(proj, lg_rows, gn_gain, *, heads, dk, dv, rows=512):
    b, s, _ = proj.shape
    assert s % rows == 0 and rows % RET_CHUNK == 0 and dv == 2 * dk
    return pl.pallas_call(
        functools.partial(_ret_kernel, rows=rows, chunk=RET_CHUNK, heads=heads),
        grid=(b, s // rows),
        in_specs=[
            _resident((heads, 1, LANES)),
            pl.BlockSpec((None, rows, heads * dk), lambda bi, c: (bi, c, 0)),
            pl.BlockSpec((None, rows, heads * dk), lambda bi, c: (bi, c, 1)),
            pl.BlockSpec((None, rows, heads * dv), lambda bi, c: (bi, c, 1)),
            pl.BlockSpec((None, rows, heads * dv), lambda bi, c: (bi, c, 2)),
            _resident((1, dv)),
        ],
        out_specs=pl.BlockSpec((None, rows, heads * dv), lambda bi, c: (bi, c, 0)),
        out_shape=jax.ShapeDtypeStruct((b, s, heads * dv), BF16),
        scratch_shapes=[pltpu.VMEM((heads, dk, dv), F32)],
        compiler_params=_params("parallel", "arbitrary"),
        name="ret_core",
    )(lg_rows, proj, proj, proj, proj, gn_gain.reshape(1, dv))


def _gdn_proj_kernel(x_ref, g_ref, w_ref, wab_ref, cw_ref, alog_ref, dtb_ref, o_ref, bg_ref,
                     h_scr, halo, *, per_seq, heads, dk):
    i = pl.program_id(0)
    tm = x_ref.shape[0]
    n_conv = cw_ref.shape[1]

    @pl.when(i == 0)
    def _():
        halo[...] = jnp.zeros_like(halo)

    h_scr[...] = _rms(x_ref[...], g_ref[...]).astype(BF16)
    first = (i % per_seq) == 0
    row8 = lax.broadcasted_iota(jnp.int32, (SUBLANES, PROJ_TILE), 0)
    conv_tiles = list(range(n_conv // PROJ_TILE))
    plain_tiles = list(range(n_conv // PROJ_TILE, w_ref.shape[1] // PROJ_TILE))
    per_plain = -(-len(conv_tiles) // max(len(plain_tiles), 1))
    order = []
    for n, tile in enumerate(conv_tiles):
        order.append(tile)
        if (n + 1) % per_plain == 0 and plain_tiles:
            order.append(plain_tiles.pop(0))
    order += plain_tiles
    for tile in order:
        c0 = tile * PROJ_TILE
        cols = slice(c0, c0 + PROJ_TILE)
        y = _dot(h_scr[...], w_ref[:, cols])
        if c0 >= n_conv:
            o_ref[:, cols] = y.astype(o_ref.dtype)
            continue
        prev = jnp.where(first, 0.0, halo[:, cols])
        halo[:, cols] = y[tm - SUBLANES:tm, :]
        for r0 in range(0, tm, CONV_ROWS):
            rows = slice(r0, r0 + CONV_ROWS)
            yc = y[rows]
            conv = cw_ref[GDN_CONV - 1:GDN_CONV, cols] * yc
            groups = jnp.concatenate([prev, yc], axis=0).reshape(CONV_ROWS // SUBLANES + 1, SUBLANES, PROJ_TILE)
            for back in range(1, GDN_CONV):
                rot = pltpu.roll(groups, back, axis=1)
                shifted = jnp.where(row8[None] < back, rot[:-1], rot[1:]).reshape(CONV_ROWS, PROJ_TILE)
                conv = conv + cw_ref[GDN_CONV - 1 - back:GDN_CONV - back, cols] * shifted
            prev = yc[CONV_ROWS - SUBLANES:]
            act = _silu(conv)
            if c0 >= 2 * heads * dk:
                o_ref[rows, cols] = act.astype(o_ref.dtype)
                continue
            scale = dk ** -0.5 if c0 < heads * dk else 1.0
            for c in range(PROJ_TILE // dk):
                a = act[:, c * dk:(c + 1) * dk]
                nrm = lax.rsqrt(jnp.sum(a * a, axis=-1, keepdims=True) + EPS) * scale
                o_ref[rows, c0 + c * dk:c0 + (c + 1) * dk] = (a * nrm).astype(o_ref.dtype)

    y = _dot(h_scr[...], wab_ref[...])
    a = y + dtb_ref[...]
    softplus = jnp.maximum(a, 0.0) + jnp.log(1.0 + jnp.exp(-jnp.abs(a)))
    log_decay = -jnp.exp(alog_ref[...]) * softplus
    beta = jax.nn.sigmoid(y)
    lane = lax.broadcasted_iota(jnp.int32, y.shape, 1)
    bg_ref[...] = jnp.where(lane < heads, log_decay, beta)


def _gdn_proj(x2, gain, w_main, w_ab, conv_w, a_log_row, dt_bias_row, *, seq, heads, dk, tm=512):
    t, d = x2.shape
    n = w_main.shape[1]
    n_conv = conv_w.shape[1]
    assert t % tm == 0 and seq % tm == 0 and n % PROJ_TILE == 0 and n_conv % PROJ_TILE == 0
    assert PROJ_TILE % dk == 0 and (heads * dk) % PROJ_TILE == 0
    per_seq = seq // tm
    return pl.pallas_call(
        functools.partial(_gdn_proj_kernel, per_seq=per_seq, heads=heads, dk=dk),
        grid=(t // tm,),
        in_specs=[
            pl.BlockSpec((tm, d), lambda i: (i, 0)),
            _resident((1, d)),
            _resident((d, n)),
            _resident((d, LANES)),
            _resident((GDN_CONV, n_conv)),
            _resident((1, LANES)),
            _resident((1, LANES)),
        ],
        out_specs=[pl.BlockSpec((tm, n), lambda i: (i, 0)), pl.BlockSpec((tm, LANES), lambda i: (i, 0))],
        out_shape=[jax.ShapeDtypeStruct((t, n), BF16), jax.ShapeDtypeStruct((t, LANES), F32)],
        scratch_shapes=[
            pltpu.VMEM((tm, d), BF16),
            pltpu.VMEM((SUBLANES, n_conv), F32),
        ],
        compiler_params=_params("arbitrary"),
        name="gdn_proj",
    )(x2, gain.reshape(1, d), w_main, w_ab, conv_w, a_log_row, dt_bias_row)


def _gdn_kernel(q_ref, k_ref, v_ref, z_ref, bg_ref, gain_ref, o_ref, state, *, rows, chunk, heads, hp):
    hb = pl.program_id(1)
    c = pl.program_id(2)
    C = chunk
    nc = rows // C
    dk = q_ref.shape[1] // hp
    dv = v_ref.shape[1] // hp

    @pl.when(c == 0)
    def _():
        state[...] = jnp.zeros_like(state)

    lane = lax.broadcasted_iota(jnp.int32, (C, LANES), 1)
    ri = lax.broadcasted_iota(jnp.int32, (C, C), 0)
    ci = lax.broadcasted_iota(jnp.int32, (C, C), 1)
    tril = ci <= ri
    strict = ci < ri
    eye = jnp.where(ci == ri, 1.0, 0.0)
    ai = lax.broadcasted_iota(jnp.int32, (2 * C, C), 0)
    am = lax.broadcasted_iota(jnp.int32, (2 * C, C), 1)
    sum_sel = jnp.where(((ai < C) & (am <= ai)) | ((ai >= C) & (am > ai - C)), 1.0, 0.0).astype(BF16)
    sum_sel2 = jnp.concatenate([sum_sel, sum_sel], axis=1)
    bm = lax.broadcasted_iota(jnp.int32, (C, 2 * C), 0)
    bj = lax.broadcasted_iota(jnp.int32, (C, 2 * C), 1)
    col_sel = jnp.where((bm > bj) | (bj >= C), 1.0, 0.0)

    items = [(n, h) for n in range(nc) for h in range(hp)]
    r_of = lambda n: slice(n * C, (n + 1) * C)
    c_of = lambda h, w: slice(h * w, (h + 1) * w)
    bgs = [bg_ref[r_of(n), :] for n in range(nc)]
    pick = lambda n, ln: jnp.sum(jnp.where(lane == ln, bgs[n], 0.0), axis=-1, keepdims=True)
    g_col = [pick(n, hb * hp + h) for n, h in items]
    b_col = [pick(n, hb * hp + h + heads) for n, h in items]

    gmat = [g * col_sel for g in g_col]
    g_hi = [m.astype(BF16) for m in gmat]
    g_split = [jnp.concatenate([hi, (m - hi.astype(F32)).astype(BF16)], axis=0) for m, hi in zip(gmat, g_hi)]
    sums = [_dot(sum_sel2, m) for m in g_split]
    decay = [jnp.where(tril, jnp.exp(s[:C, :C]), 0.0) for s in sums]
    g_cum = [s[:C, C:C + 1] for s in sums]
    exp_g = [jnp.exp(g) for g in g_cum]
    exp_rem = [jnp.exp(s[C:, C:C + 1]) for s in sums]
    exp_last = [e[C - 1:C, :] for e in exp_g]

    q = [q_ref[r_of(n), c_of(h, dk)] for n, h in items]
    k = [k_ref[r_of(n), c_of(h, dk)] for n, h in items]
    kf = [x.astype(F32) for x in k]
    k_beta = [x * b for x, b in zip(kf, b_col)]
    kq = [_dot_nt(jnp.concatenate([kb.astype(BF16), a], axis=0), x) for kb, a, x in zip(k_beta, q, k)]
    n_mat = [jnp.where(strict, m[:C] * d, 0.0) for m, d in zip(kq, decay)]
    attn = [jnp.where(tril, m[C:] * d, 0.0).astype(BF16) for m, d in zip(kq, decay)]
    inv = [eye - m for m in n_mat]
    n_bf = [m.astype(BF16) for m in n_mat]
    power = [_dot(m, m) for m in n_bf]
    for _ in range(4):
        p_bf = [p.astype(BF16) for p in power]
        both = [_dot(jnp.concatenate([x.astype(BF16), p], axis=0), p) for x, p in zip(inv, p_bf)]
        inv = [x + m[:C] for x, m in zip(inv, both)]
        power = [m[C:] for m in both]
    inv = [x + _dot(x.astype(BF16), p.astype(BF16)) for x, p in zip(inv, power)]

    rhs = [jnp.concatenate([v_ref[r_of(n), c_of(h, dv)].astype(F32) * b, kb * eg], axis=1).astype(BF16)
           for (n, h), b, kb, eg in zip(items, b_col, k_beta, exp_g)]
    uw = [_dot(x.astype(BF16), r) for x, r in zip(inv, rhs)]
    kd = [(x * e).astype(BF16) for x, e in zip(kf, exp_rem)]

    st = [state[h] for h in range(hp)]
    for n in range(nc):
        idx = [n * hp + h for h in range(hp)]
        wq = [jnp.concatenate([uw[i][:, dv:].astype(BF16), q[i]], axis=0) for i in idx]
        ws = [_dot(a, s.astype(BF16)) for a, s in zip(wq, st)]
        v_new = [(uw[i][:, :dv] - w[:C]).astype(BF16) for i, w in zip(idx, ws)]
        o = [exp_g[i] * w[C:] + _dot(attn[i], vn) for i, w, vn in zip(idx, ws, v_new)]
        st = [s * exp_last[i] + _dot_tn(kd[i], vn) for i, s, vn in zip(idx, st, v_new)]
        for h in range(hp):
            gate = z_ref[r_of(n), c_of(h, dv)].astype(F32)
            o_ref[r_of(n), c_of(h, dv)] = (_rms(o[h], gain_ref[...]) * _silu(gate)).astype(o_ref.dtype)
    for h in range(hp):
        state[h] = st[h]


def _gdn_core(proj, bg, norm_gain, *, heads, dk, dv, rows=512, hp=8):
    b, s, _ = proj.shape
    assert s % rows == 0 and rows % GDN_CHUNK == 0 and dv == 2 * dk and GDN_CHUNK == 64
    assert heads % hp == 0
    nhb = heads // hp
    return pl.pallas_call(
        functools.partial(_gdn_kernel, rows=rows, chunk=GDN_CHUNK, heads=heads, hp=hp),
        grid=(b, nhb, s // rows),
        in_specs=[
            pl.BlockSpec((None, rows, hp * dk), lambda bi, h, c: (bi, c, h)),
            pl.BlockSpec((None, rows, hp * dk), lambda bi, h, c: (bi, c, nhb + h)),
            pl.BlockSpec((None, rows, hp * dv), lambda bi, h, c: (bi, c, nhb + h)),
            pl.BlockSpec((None, rows, hp * dv), lambda bi, h, c: (bi, c, 2 * nhb + h)),
            pl.BlockSpec((None, rows, LANES), lambda bi, h, c: (bi, c, 0)),
            pl.BlockSpec((1, dv), lambda bi, h, c: (0, 0)),
        ],
        out_specs=pl.BlockSpec((None, rows, hp * dv), lambda bi, h, c: (bi, c, h)),
        out_shape=jax.ShapeDtypeStruct((b, s, heads * dv), BF16),
        scratch_shapes=[pltpu.VMEM((hp, dk, dv), F32)],
        compiler_params=_params("parallel", "parallel", "arbitrary"),
        name="gdn_core",
    )(proj, proj, proj, proj, bg, norm_gain.reshape(1, dv))


def _sb_mixer(x, gain, w_qkv, q_gain, k_gain):
    b, s, d = x.shape
    heads = w_qkv.shape[1] // (3 * SB_HEAD_DIM)
    qkv = _sb_proj(x.reshape(b * s, d), gain, w_qkv.astype(BF16), q_gain, k_gain)
    return _sb_attn(qkv.reshape(b, s, -1), heads=heads)


def _ret_mixer(x, gain, w_in, gn_gain):
    b, s, d = x.shape
    heads = RET_HEADS
    dv = gn_gain.shape[0]
    dk = dv // 2
    half = dk // 2
    inv = ROPE_BASE ** (-jnp.arange(half, dtype=F32) / half)
    ang = jnp.arange(s, dtype=F32)[:, None] * inv[None, :]
    log_gamma = jnp.log1p(-jnp.exp2(-5.0 - jnp.arange(heads, dtype=F32)))
    lg_rows = jnp.broadcast_to(log_gamma[:, None, None], (heads, 1, LANES))
    proj = _ret_proj(x.reshape(b * s, d), gain, w_in.astype(BF16), jnp.cos(ang), jnp.sin(ang),
                     seq=s, dk=dk, heads=heads)
    return _ret_core(proj.reshape(b, s, -1), lg_rows, gn_gain, heads=heads, dk=dk, dv=dv)


def _gdn_mixer(x, gain, w_in, conv_w, a_log, dt_bias, norm_gain):
    b, s, d = x.shape
    heads = a_log.shape[0]
    dk = GDN_K_DIM
    dv = norm_gain.shape[0]
    n_main = 2 * heads * dk + 2 * heads * dv
    w_ab = jnp.zeros((d, LANES), BF16).at[:, :2 * heads].set(w_in[:, n_main:].astype(BF16))
    pad = lambda p: jnp.zeros((1, LANES), F32).at[0, :heads].set(p)
    proj, bg = _gdn_proj(x.reshape(b * s, d), gain, w_in[:, :n_main].astype(BF16), w_ab, conv_w,
                         pad(a_log), pad(dt_bias), seq=s, heads=heads, dk=dk)
    return _gdn_core(proj.reshape(b, s, -1), bg.reshape(b, s, LANES), norm_gain, heads=heads, dk=dk, dv=dv)


def _ffn_layer(x, gain, w_in, w_out, *, o=None, w_o=None, pool=None):
    b, s, d = x.shape
    proj = None if o is None else (o.reshape(b * s, -1), w_o.astype(BF16))
    out = _ffn(x.reshape(b * s, d), gain, w_in.astype(BF16), w_out.astype(BF16), seq=s, proj=proj, pool=pool)
    return out.reshape(b, s, d)


def kernel(x, norm_mix, norm_ffn, ffn_w_in, ffn_w_out, pool_w, pool_scale, sb_w_qkv, sb_q_gain, sb_k_gain, sb_w_o, ret_w_in, ret_gn_gain, ret_w_o, gdn_w_in, gdn_conv_w, gdn_a_log, gdn_dt_bias, gdn_norm_gain, gdn_w_o):
    depth = norm_mix.shape[0]
    for layer in range(depth):
        mixer, occ = layer % 4, layer // 4
        ffn = functools.partial(_ffn_layer, x, norm_ffn[layer], ffn_w_in[layer], ffn_w_out[layer])
        if mixer == 0:
            x = ffn(pool=(norm_mix[layer], pool_w[occ].astype(BF16), pool_scale[occ]))
        elif mixer == 1:
            x = ffn(o=_sb_mixer(x, norm_mix[layer], sb_w_qkv[occ], sb_q_gain[occ], sb_k_gain[occ]), w_o=sb_w_o[occ])
        elif mixer == 2:
            x = ffn(o=_ret_mixer(x, norm_mix[layer], ret_w_in[occ], ret_gn_gain[occ]), w_o=ret_w_o[occ])
        else:
            x = ffn(o=_gdn_mixer(x, norm_mix[layer], gdn_w_in[occ], gdn_conv_w[occ], gdn_a_log[occ],
                                 gdn_dt_bias[occ], gdn_norm_gain[occ]), w_o=gdn_w_o[occ])
    return x
```

```python
import functools

import jax
import jax.numpy as jnp
from jax import lax
from jax.experimental import pallas as pl
from jax.experimental.pallas import tpu as pltpu

F32 = jnp.float32
BF16 = jnp.bfloat16

EPS = 1e-6
LOG2E = 1.4426950408889634
SOFTPLUS_LINEAR = 64.0
UNDERFLOW_LOG2 = 160.0
ROPE_BASE = 10000.0
LANES = 128
SUBLANES = 8
VMEM_LIMIT = 56 * 1024 * 1024
PROJ_TILE = 256

POOL_WINDOWS = (2, 4, 8, 16)
POOL_HALO = 32
SB_HEAD_DIM = 128
RET_HEADS = 4
RET_CHUNK = 256
GDN_K_DIM = 128
GDN_CONV = 4
GDN_CHUNK = 64


def _params(*sem):
    return pltpu.CompilerParams(dimension_semantics=sem, vmem_limit_bytes=VMEM_LIMIT)


def _resident(shape):
    return pl.BlockSpec(shape, lambda *_: (0,) * len(shape), pipeline_mode=pl.Buffered(1))


def _rms(xf, gain_row):
    ms = jnp.mean(xf * xf, axis=-1, keepdims=True)
    return xf * lax.rsqrt(ms + EPS) * gain_row


def _silu(x):
    h = 0.5 * x
    return h + h * jnp.tanh(h)


def _dot(a, b):
    return jnp.dot(a, b, preferred_element_type=F32)


def _dot_nt(a, b):
    return lax.dot_general(a, b, (((1,), (1,)), ((), ())), preferred_element_type=F32)


def _dot_tn(a, b):
    return lax.dot_general(a, b, (((0,), (0,)), ((), ())), preferred_element_type=F32)


def _pool_residual(x, seq_tile, g_ref, w_ref, sc_ref, ext, buf_a, buf_b, dst_ref):
    ts, d = x.shape
    n = ts + POOL_HALO
    group = w_ref.shape[-1]

    @pl.when(seq_tile == 0)
    def _():
        ext[0:POOL_HALO, :] = jnp.zeros((POOL_HALO, d), F32)

    ext[POOL_HALO:n, :] = _rms(x, g_ref[...])

    src, dst = ext, buf_a
    for level in range(1, len(POOL_WINDOWS) + 1):
        shift = 1 << (level - 1)
        lo = SUBLANES * level
        c0 = (level - 1) * group
        dst[lo:n, c0:d] = src[lo:n, c0:d] + src[lo - shift:n - shift, c0:d]
        src, dst = dst, (buf_b if dst is buf_a else buf_a)

    pos1 = (seq_tile * ts + lax.broadcasted_iota(jnp.int32, (ts, 1), 0) + 1).astype(F32)
    for g, win in enumerate(POOL_WINDOWS):
        level_buf = buf_a if g % 2 == 0 else buf_b
        cols = slice(g * group, (g + 1) * group)
        inv_cnt = 1.0 / jnp.minimum(pos1, float(win))
        pooled = level_buf[POOL_HALO:n, cols] * inv_cnt - ext[POOL_HALO:n, cols]
        mixed = _dot(pooled.astype(BF16), w_ref[g])
        dst_ref[:, cols] = x[:, cols] + mixed * sc_ref[:, cols]

    ext[0:POOL_HALO, :] = ext[ts:n, :]


def _ffn_kernel(*refs, mixer, per_seq):
    if mixer == "proj":
        x_ref, a_ref, wo_ref, g_ref, wi_ref, wd_ref, o_ref, h_scr, act_scr, x1_scr = refs
        x1_scr[...] = x_ref[...] + _dot(a_ref[...], wo_ref[...])
    else:
        x_ref, pg_ref, pw_ref, psc_ref, g_ref, wi_ref, wd_ref, o_ref, h_scr, act_scr, x1_scr, ext, buf_a, buf_b = refs
        _pool_residual(x_ref[...], pl.program_id(0) % per_seq, pg_ref, pw_ref, psc_ref, ext, buf_a, buf_b, x1_scr)
    h_scr[...] = _rms(x1_scr[...], g_ref[...]).astype(BF16)
    hidden = wd_ref.shape[0]
    for f0 in range(0, hidden, PROJ_TILE):
        gate = _dot(h_scr[...], wi_ref[:, f0:f0 + PROJ_TILE])
        up = _dot(h_scr[...], wi_ref[:, hidden + f0:hidden + f0 + PROJ_TILE])
        act_scr[:, f0:f0 + PROJ_TILE] = (_silu(gate) * up).astype(BF16)
    for c0 in range(0, o_ref.shape[1], PROJ_TILE):
        cols = slice(c0, c0 + PROJ_TILE)
        o_ref[:, cols] = x1_scr[:, cols] + _dot(act_scr[...], wd_ref[:, cols])


def _ffn(x2, gain, w_in, w_out, *, seq, proj=None, pool=None, tm=512):
    t, d = x2.shape
    hidden = w_out.shape[0]
    assert t % tm == 0 and seq % tm == 0 and hidden % PROJ_TILE == 0 and d % PROJ_TILE == 0
    assert (proj is None) != (pool is None)
    row = lambda width: pl.BlockSpec((tm, width), lambda i: (i, 0))
    scratch = [pltpu.VMEM((tm, d), BF16), pltpu.VMEM((tm, hidden), BF16), pltpu.VMEM((tm, d), F32)]
    if proj is not None:
        a2, w_o = proj
        in_specs = [row(d), row(a2.shape[1]), _resident(w_o.shape)]
        args = [x2, a2, w_o]
    else:
        pool_gain, w_group, scale = pool
        assert w_group.shape[0] == len(POOL_WINDOWS) and w_group.shape[0] * w_group.shape[1] == d
        in_specs = [row(d), _resident((1, d)), _resident(w_group.shape), _resident((1, d))]
        args = [x2, pool_gain.reshape(1, d), w_group, scale.reshape(1, d)]
        scratch += [pltpu.VMEM((tm + POOL_HALO, d), F32)] * 3
    in_specs += [_resident((1, d)), _resident(w_in.shape), _resident(w_out.shape)]
    args += [gain.reshape(1, d), w_in, w_out]
    return pl.pallas_call(
        functools.partial(_ffn_kernel, mixer="proj" if proj is not None else "pool", per_seq=seq // tm),
        grid=(t // tm,),
        in_specs=in_specs,
        out_specs=row(d),
        out_shape=jax.ShapeDtypeStruct((t, d), F32),
        scratch_shapes=scratch,
        compiler_params=_params("parallel" if proj is not None else "arbitrary"),
        name="ffn",
    )(*args)


def _sb_proj_kernel(x_ref, g_ref, w_ref, qg_ref, kg_ref, o_ref, h_scr, *, width):
    h_scr[...] = _rms(x_ref[...], g_ref[...]).astype(BF16)
    gains = (qg_ref[...] * (SB_HEAD_DIM ** -0.5 * LOG2E), kg_ref[...], None)
    for tile in range(w_ref.shape[1] // PROJ_TILE):
        cols = slice(tile * PROJ_TILE, (tile + 1) * PROJ_TILE)
        y = _dot(h_scr[...], w_ref[:, cols])
        gain = gains[tile * PROJ_TILE // width]
        if gain is None:
            o_ref[:, cols] = y.astype(o_ref.dtype)
        else:
            for c in range(PROJ_TILE // SB_HEAD_DIM):
                sub = slice(c * SB_HEAD_DIM, (c + 1) * SB_HEAD_DIM)
                o_ref[:, tile * PROJ_TILE + c * SB_HEAD_DIM:tile * PROJ_TILE + (c + 1) * SB_HEAD_DIM] = (
                    _rms(y[:, sub], gain).astype(o_ref.dtype))


def _sb_proj(x2, gain, w_qkv, q_gain, k_gain, *, tm=512):
    t, d = x2.shape
    n = w_qkv.shape[1]
    width = n // 3
    assert t % tm == 0 and width % PROJ_TILE == 0 and PROJ_TILE % SB_HEAD_DIM == 0
    return pl.pallas_call(
        functools.partial(_sb_proj_kernel, width=width),
        grid=(t // tm,),
        in_specs=[
            pl.BlockSpec((tm, d), lambda i: (i, 0)),
            _resident((1, d)),
            _resident((d, n)),
            _resident((1, SB_HEAD_DIM)),
            _resident((1, SB_HEAD_DIM)),
        ],
        out_specs=pl.BlockSpec((tm, n), lambda i: (i, 0)),
        out_shape=jax.ShapeDtypeStruct((t, n), BF16),
        scratch_shapes=[pltpu.VMEM((tm, d), BF16)],
        compiler_params=_params("parallel"),
        name="sb_proj",
    )(x2, gain.reshape(1, d), w_qkv, q_gain.reshape(1, -1), k_gain.reshape(1, -1))


def _sb_attn_kernel(q_ref, k_ref, v_ref, o_ref, *, tq, unroll):
    i = pl.program_id(2)
    tk = tq
    dh = SB_HEAD_DIM
    hp = q_ref.shape[1] // dh
    lanes = [slice(h * dh, (h + 1) * dh) for h in range(hp)]
    q = [q_ref[:, ln] for ln in lanes]
    row = lax.broadcasted_iota(jnp.int32, (tq, tk), 0)
    col = lax.broadcasted_iota(jnp.int32, (tq, tk), 1)
    causal = col < row
    jj = lax.broadcasted_iota(jnp.int32, (tk, tk), 0)
    ss = lax.broadcasted_iota(jnp.int32, (tk, tk), 1)
    suffix = jnp.where(jj > ss, 1.0, 0.0).astype(BF16)

    def scores(h, kb, keep):
        start = pl.multiple_of(kb * tk, tk)
        z = _dot_nt(q[h], k_ref[pl.ds(start, tk), lanes[h]])
        p = jnp.where(z > SOFTPLUS_LINEAR, z, jnp.log(1.0 + jnp.exp2(z)) * LOG2E)
        log_beta = z - p
        if keep is not None:
            p = jnp.where(keep, p, 0.0)
        return p.astype(BF16), log_beta, p[:, :1]

    def blocks(kbs, keeps, acc, run):
        acc, run = list(acc), list(run)
        stage1 = [[scores(h, kb, keep) for kb, keep in zip(kbs, keeps)] for h in range(hp)]
        afters = [[_dot(p_bf, suffix) for p_bf, _, _ in per_head] for per_head in stage1]
        for n, (kb, keep) in enumerate(zip(kbs, keeps)):
            start = pl.multiple_of(kb * tk, tk)
            for h in range(hp):
                _, log_beta, first_col = stage1[h][n]
                after = afters[h][n]
                wts = jnp.exp2(log_beta - (after + run[h]))
                if keep is not None:
                    wts = jnp.where(keep, wts, 0.0)
                acc[h] = acc[h] + _dot(wts.astype(BF16), v_ref[pl.ds(start, tk), lanes[h]])
                run[h] = run[h] + (after[:, :1] + first_col)
        return tuple(acc), tuple(run)

    def live(run):
        return functools.reduce(jnp.minimum, [jnp.min(r) for r in run]) < UNDERFLOW_LOG2

    def head(n_extra):
        def branch():
            acc = tuple(jnp.zeros((tq, dh), F32) for _ in range(hp))
            run = tuple(jnp.zeros((tq, 1), F32) for _ in range(hp))
            kbs = [i] + [i - 1 - u for u in range(n_extra)]
            return blocks(kbs, [causal] + [None] * n_extra, acc, run)
        return branch

    n_groups = (jnp.maximum(i - 1, 0) + unroll - 1) // unroll

    def group(carry):
        t, acc, run, _ = carry
        kbs = [i - 2 - unroll * t - u for u in range(unroll)]
        acc, run = blocks([jnp.maximum(kb, 0) for kb in kbs], [kb >= 0 for kb in kbs], acc, run)
        return t + 1, acc, run, live(run)

    acc, run = lax.switch(jnp.minimum(i, 1), [head(0), head(1)])
    _, acc, run, _ = lax.while_loop(lambda c: (c[0] < n_groups) & c[3], group,
                                    (jnp.int32(0), acc, run, live(run)))
    for h in range(hp):
        o_ref[:, lanes[h]] = acc[h].astype(o_ref.dtype)


def _sb_attn(qkv, *, heads, tq=256, unroll=2, hp=4):
    b, s, _ = qkv.shape
    width = hp * SB_HEAD_DIM
    nhb = heads // hp
    assert s % tq == 0 and heads % hp == 0
    return pl.pallas_call(
        functools.partial(_sb_attn_kernel, tq=tq, unroll=unroll),
        grid=(b, nhb, s // tq),
        in_specs=[
            pl.BlockSpec((None, tq, width), lambda bi, h, i: (bi, i, h)),
            pl.BlockSpec((None, s, width), lambda bi, h, i: (bi, 0, nhb + h)),
            pl.BlockSpec((None, s, width), lambda bi, h, i: (bi, 0, 2 * nhb + h)),
        ],
        out_specs=pl.BlockSpec((None, tq, width), lambda bi, h, i: (bi, i, h)),
        out_shape=jax.ShapeDtypeStruct((b, s, heads * SB_HEAD_DIM), BF16),
        compiler_params=_params("parallel", "parallel", "arbitrary"),
        name="sb_attn",
    )(qkv, qkv, qkv)


def _ret_proj_kernel(x_ref, g_ref, w_ref, cos_ref, sin_ref, o_ref, h_scr, *, dk, heads):
    h_scr[...] = _rms(x_ref[...], g_ref[...]).astype(BF16)
    half = dk // 2
    cos = cos_ref[...]
    sin = sin_ref[...]
    for tile in range(w_ref.shape[1] // dk):
        c0 = tile * dk
        y = _dot(h_scr[...], w_ref[:, c0:c0 + dk])
        if tile < 2 * heads:
            if tile >= heads:
                y = y * (dk ** -0.5)
            x1 = y[:, :half]
            x2 = y[:, half:]
            o_ref[:, c0:c0 + half] = (x1 * cos - x2 * sin).astype(o_ref.dtype)
            o_ref[:, c0 + half:c0 + dk] = (x1 * sin + x2 * cos).astype(o_ref.dtype)
        else:
            o_ref[:, c0:c0 + dk] = y.astype(o_ref.dtype)


def _ret_proj(x2, gain, w_in, cos, sin, *, seq, dk, heads, tm=512):
    t, d = x2.shape
    n = w_in.shape[1]
    assert t % tm == 0 and seq % tm == 0 and n % dk == 0 and dk == PROJ_TILE
    per_seq = seq // tm
    return pl.pallas_call(
        functools.partial(_ret_proj_kernel, dk=dk, heads=heads),
        grid=(t // tm,),
        in_specs=[
            pl.BlockSpec((tm, d), lambda i: (i, 0)),
            _resident((1, d)),
            _resident((d, n)),
            pl.BlockSpec((tm, dk // 2), lambda i: (i % per_seq, 0)),
            pl.BlockSpec((tm, dk // 2), lambda i: (i % per_seq, 0)),
        ],
        out_specs=pl.BlockSpec((tm, n), lambda i: (i, 0)),
        out_shape=jax.ShapeDtypeStruct((t, n), BF16),
        scratch_shapes=[pltpu.VMEM((tm, d), BF16)],
        compiler_params=_params("parallel"),
        name="ret_proj",
    )(x2, gain.reshape(1, d), w_in, cos, sin)


def _ret_kernel(lg_ref, q_ref, k_ref, v_ref, g_ref, gn_ref, o_ref, state, *, rows, chunk, heads):
    c = pl.program_id(1)
    dk = q_ref.shape[1] // heads
    dv = v_ref.shape[1] // heads

    @pl.when(c == 0)
    def _():
        state[...] = jnp.zeros_like(state)

    ri = lax.broadcasted_iota(jnp.int32, (chunk, chunk), 0)
    ci = lax.broadcasted_iota(jnp.int32, (chunk, chunk), 1)
    diff = (ri - ci).astype(F32)
    idx = lax.broadcasted_iota(jnp.int32, (chunk, 1), 0).astype(F32)
    lg = [lg_ref[h][:, :1] for h in range(heads)]
    intra = [jnp.where(diff >= 0, jnp.exp(jnp.maximum(diff, 0.0) * l), 0.0) for l in lg]
    q_dec = [jnp.exp((idx + 1.0) * l) for l in lg]
    k_dec = [jnp.exp((chunk - 1.0 - idx) * l) for l in lg]
    chunk_dec = [jnp.exp(chunk * l) for l in lg]

    st = [state[h] for h in range(heads)]
    for n in range(rows // chunk):
        r = slice(n * chunk, (n + 1) * chunk)
        q = [q_ref[r, h * dk:(h + 1) * dk] for h in range(heads)]
        k = [k_ref[r, h * dk:(h + 1) * dk] for h in range(heads)]
        v = [v_ref[r, h * dv:(h + 1) * dv] for h in range(heads)]
        scores = [(_dot_nt(a, b) * m).astype(BF16) for a, b, m in zip(q, k, intra)]
        kd = [(b.astype(F32) * d).astype(BF16) for b, d in zip(k, k_dec)]
        inter = [_dot(a, s.astype(BF16)) for a, s in zip(q, st)]
        kv = [_dot_tn(a, b) for a, b in zip(kd, v)]
        o = [_dot(sc, b) + d * it for sc, b, d, it in zip(scores, v, q_dec, inter)]
        st = [s * d + u for s, d, u in zip(st, chunk_dec, kv)]
        for h in range(heads):
            gate = g_ref[r, h * dv:(h + 1) * dv].astype(F32)
            o_ref[r, h * dv:(h + 1) * dv] = (_rms(o[h], gn_ref[...]) * _silu(gate)).astype(o_ref.dtype)
    for h in range(heads):
        state[h] = st[h]


def _ret_core(proj, lg_rows, gn_gain, *, heads, dk, dv, rows=512):
    b, s, _ = proj.shape
    assert s % rows == 0 and rows % RET_CHUNK == 0 and dv == 2 * dk
    return pl.pallas_call(
        functools.partial(_ret_kernel, rows=rows, chunk=RET_CHUNK, heads=heads),
        grid=(b, s // rows),
        in_specs=[
            _resident((heads, 1, LANES)),
            pl.BlockSpec((None, rows, heads * dk), lambda bi, c: (bi, c, 0)),
            pl.BlockSpec((None, rows, heads * dk), lambda bi, c: (bi, c, 1)),
            pl.BlockSpec((None, rows, heads * dv), lambda bi, c: (bi, c, 1)),
            pl.BlockSpec((None, rows, heads * dv), lambda bi, c: (bi, c, 2)),
            _resident((1, dv)),
        ],
        out_specs=pl.BlockSpec((None, rows, heads * dv), lambda bi, c: (bi, c, 0)),
        out_shape=jax.ShapeDtypeStruct((b, s, heads * dv), BF16),
        scratch_shapes=[pltpu.VMEM((heads, dk, dv), F32)],
        compiler_params=_params("parallel", "arbitrary"),
        name="ret_core",
    )(lg_rows, proj, proj, proj, proj, gn_gain.reshape(1, dv))


def _gdn_proj_kernel(x_ref, g_ref, w_ref, wab_ref, cw_ref, alog_ref, dtb_ref, o_ref, bg_ref,
                     h_scr, halo, *, per_seq, heads, dk):
    i = pl.program_id(0)
    tm = x_ref.shape[0]
    n_conv = cw_ref.shape[1]

    @pl.when(i == 0)
    def _():
        halo[...] = jnp.zeros_like(halo)

    h_scr[...] = _rms(x_ref[...], g_ref[...]).astype(BF16)
    first = (i % per_seq) == 0
    row8 = lax.broadcasted_iota(jnp.int32, (SUBLANES, PROJ_TILE), 0)
    conv_tiles = list(range(n_conv // PROJ_TILE))
    plain_tiles = list(range(n_conv // PROJ_TILE, w_ref.shape[1] // PROJ_TILE))
    per_plain = -(-len(conv_tiles) // max(len(plain_tiles), 1))
    order = []
    for n, tile in enumerate(conv_tiles):
        order.append(tile)
        if (n + 1) % per_plain == 0 and plain_tiles:
            order.append(plain_tiles.pop(0))
    order += plain_tiles
    for tile in order:
        c0 = tile * PROJ_TILE
        cols = slice(c0, c0 + PROJ_TILE)
        y = _dot(h_scr[...], w_ref[:, cols])
        if c0 >= n_conv:
            o_ref[:, cols] = y.astype(o_ref.dtype)
            continue
        prev = jnp.where(first, 0.0, halo[:, cols])
        halo[:, cols] = y[tm - SUBLANES:tm, :]
        conv = cw_ref[GDN_CONV - 1:GDN_CONV, cols] * y
        groups = jnp.concatenate([prev, y], axis=0).reshape(tm // SUBLANES + 1, SUBLANES, PROJ_TILE)
        for back in range(1, GDN_CONV):
            rot = pltpu.roll(groups, back, axis=1)
            shifted = jnp.where(row8[None] < back, rot[:-1], rot[1:]).reshape(tm, PROJ_TILE)
            conv = conv + cw_ref[GDN_CONV - 1 - back:GDN_CONV - back, cols] * shifted
        act = _silu(conv)
        if c0 >= 2 * heads * dk:
            o_ref[:, cols] = act.astype(o_ref.dtype)
            continue
        scale = dk ** -0.5 if c0 < heads * dk else 1.0
        for c in range(PROJ_TILE // dk):
            a = act[:, c * dk:(c + 1) * dk]
            nrm = lax.rsqrt(jnp.sum(a * a, axis=-1, keepdims=True) + EPS) * scale
            o_ref[:, c0 + c * dk:c0 + (c + 1) * dk] = (a * nrm).astype(o_ref.dtype)

    y = _dot(h_scr[...], wab_ref[...])
    a = y + dtb_ref[...]
    softplus = jnp.maximum(a, 0.0) + jnp.log(1.0 + jnp.exp(-jnp.abs(a)))
    log_decay = -jnp.exp(alog_ref[...]) * softplus
    beta = jax.nn.sigmoid(y)
    lane = lax.broadcasted_iota(jnp.int32, y.shape, 1)
    bg_ref[...] = jnp.where(lane < heads, log_decay, beta)


def _gdn_proj(x2, gain, w_main, w_ab, conv_w, a_log_row, dt_bias_row, *, seq, heads, dk, tm=512):
    t, d = x2.shape
    n = w_main.shape[1]
    n_conv = conv_w.shape[1]
    assert t % tm == 0 and seq % tm == 0 and n % PROJ_TILE == 0 and n_conv % PROJ_TILE == 0
    assert PROJ_TILE % dk == 0 and (heads * dk) % PROJ_TILE == 0
    per_seq = seq // tm
    return pl.pallas_call(
        functools.partial(_gdn_proj_kernel, per_seq=per_seq, heads=heads, dk=dk),
        grid=(t // tm,),
        in_specs=[
            pl.BlockSpec((tm, d), lambda i: (i, 0)),
            _resident((1, d)),
            _resident((d, n)),
            _resident((d, LANES)),
            _resident((GDN_CONV, n_conv)),
            _resident((1, LANES)),
            _resident((1, LANES)),
        ],
        out_specs=[pl.BlockSpec((tm, n), lambda i: (i, 0)), pl.BlockSpec((tm, LANES), lambda i: (i, 0))],
        out_shape=[jax.ShapeDtypeStruct((t, n), BF16), jax.ShapeDtypeStruct((t, LANES), F32)],
        scratch_shapes=[
            pltpu.VMEM((tm, d), BF16),
            pltpu.VMEM((SUBLANES, n_conv), F32),
        ],
        compiler_params=_params("arbitrary"),
        name="gdn_proj",
    )(x2, gain.reshape(1, d), w_main, w_ab, conv_w, a_log_row, dt_bias_row)


def _gdn_kernel(q_ref, k_ref, v_ref, z_ref, bg_ref, gain_ref, o_ref, state, *, rows, chunk, heads, hp):
    hb = pl.program_id(1)
    c = pl.program_id(2)
    C = chunk
    nc = rows // C
    dk = q_ref.shape[1] // hp
    dv = v_ref.shape[1] // hp

    @pl.when(c == 0)
    def _():
        state[...] = jnp.zeros_like(state)

    lane = lax.broadcasted_iota(jnp.int32, (C, LANES), 1)
    ri = lax.broadcasted_iota(jnp.int32, (C, C), 0)
    ci = lax.broadcasted_iota(jnp.int32, (C, C), 1)
    tril = ci <= ri
    strict = ci < ri
    eye = jnp.where(ci == ri, 1.0, 0.0)
    ai = lax.broadcasted_iota(jnp.int32, (2 * C, C), 0)
    am = lax.broadcasted_iota(jnp.int32, (2 * C, C), 1)
    sum_sel = jnp.where(((ai < C) & (am <= ai)) | ((ai >= C) & (am > ai - C)), 1.0, 0.0).astype(BF16)
    sum_sel2 = jnp.concatenate([sum_sel, sum_sel], axis=1)
    bm = lax.broadcasted_iota(jnp.int32, (C, 2 * C), 0)
    bj = lax.broadcasted_iota(jnp.int32, (C, 2 * C), 1)
    col_sel = jnp.where((bm > bj) | (bj >= C), 1.0, 0.0)

    items = [(n, h) for n in range(nc) for h in range(hp)]
    r_of = lambda n: slice(n * C, (n + 1) * C)
    c_of = lambda h, w: slice(h * w, (h + 1) * w)
    bgs = [bg_ref[r_of(n), :] for n in range(nc)]
    pick = lambda n, ln: jnp.sum(jnp.where(lane == ln, bgs[n], 0.0), axis=-1, keepdims=True)
    g_col = [pick(n, hb * hp + h) for n, h in items]
    b_col = [pick(n, hb * hp + h + heads) for n, h in items]

    gmat = [g * col_sel for g in g_col]
    g_hi = [m.astype(BF16) for m in gmat]
    g_split = [jnp.concatenate([hi, (m - hi.astype(F32)).astype(BF16)], axis=0) for m, hi in zip(gmat, g_hi)]
    sums = [_dot(sum_sel2, m) for m in g_split]
    decay = [jnp.where(tril, jnp.exp(s[:C, :C]), 0.0) for s in sums]
    g_cum = [s[:C, C:C + 1] for s in sums]
    exp_g = [jnp.exp(g) for g in g_cum]
    exp_rem = [jnp.exp(s[C:, C:C + 1]) for s in sums]
    exp_last = [e[C - 1:C, :] for e in exp_g]

    q = [q_ref[r_of(n), c_of(h, dk)] for n, h in items]
    k = [k_ref[r_of(n), c_of(h, dk)] for n, h in items]
    kf = [x.astype(F32) for x in k]
    k_beta = [x * b for x, b in zip(kf, b_col)]
    kq = [_dot_nt(jnp.concatenate([kb.astype(BF16), a], axis=0), x) for kb, a, x in zip(k_beta, q, k)]
    n_mat = [jnp.where(strict, m[:C] * d, 0.0) for m, d in zip(kq, decay)]
    attn = [jnp.where(tril, m[C:] * d, 0.0).astype(BF16) for m, d in zip(kq, decay)]
    inv = [eye - m for m in n_mat]
    n_bf = [m.astype(BF16) for m in n_mat]
    power = [_dot(m, m) for m in n_bf]
    for _ in range(4):
        p_bf = [p.astype(BF16) for p in power]
        both = [_dot(jnp.concatenate([x.astype(BF16), p], axis=0), p) for x, p in zip(inv, p_bf)]
        inv = [x + m[:C] for x, m in zip(inv, both)]
        power = [m[C:] for m in both]
    inv = [x + _dot(x.astype(BF16), p.astype(BF16)) for x, p in zip(inv, power)]

    rhs = [jnp.concatenate([v_ref[r_of(n), c_of(h, dv)].astype(F32) * b, kb * eg], axis=1).astype(BF16)
           for (n, h), b, kb, eg in zip(items, b_col, k_beta, exp_g)]
    uw = [_dot(x.astype(BF16), r) for x, r in zip(inv, rhs)]
    kd = [(x * e).astype(BF16) for x, e in zip(kf, exp_rem)]

    st = [state[h] for h in range(hp)]
    for n in range(nc):
        idx = [n * hp + h for h in range(hp)]
        wq = [jnp.concatenate([uw[i][:, dv:].astype(BF16), q[i]], axis=0) for i in idx]
        ws = [_dot(a, s.astype(BF16)) for a, s in zip(wq, st)]
        v_new = [(uw[i][:, :dv] - w[:C]).astype(BF16) for i, w in zip(idx, ws)]
        o = [exp_g[i] * w[C:] + _dot(attn[i], vn) for i, w, vn in zip(idx, ws, v_new)]
        st = [s * exp_last[i] + _dot_tn(kd[i], vn) for i, s, vn in zip(idx, st, v_new)]
        for h in range(hp):
            gate = z_ref[r_of(n), c_of(h, dv)].astype(F32)
            o_ref[r_of(n), c_of(h, dv)] = (_rms(o[h], gain_ref[...]) * _silu(gate)).astype(o_ref.dtype)
    for h in range(hp):
        state[h] = st[h]


def _gdn_core(proj, bg, norm_gain, *, heads, dk, dv, rows=512, hp=8):
    b, s, _ = proj.shape
    assert s % rows == 0 and rows % GDN_CHUNK == 0 and dv == 2 * dk and GDN_CHUNK == 64
    assert heads % hp == 0
    nhb = heads // hp
    return pl.pallas_call(
        functools.partial(_gdn_kernel, rows=rows, chunk=GDN_CHUNK, heads=heads, hp=hp),
        grid=(b, nhb, s // rows),
        in_specs=[
            pl.BlockSpec((None, rows, hp * dk), lambda bi, h, c: (bi, c, h)),
            pl.BlockSpec((None, rows, hp * dk), lambda bi, h, c: (bi, c, nhb + h)),
            pl.BlockSpec((None, rows, hp * dv), lambda bi, h, c: (bi, c, nhb + h)),
            pl.BlockSpec((None, rows, hp * dv), lambda bi, h, c: (bi, c, 2 * nhb + h)),
            pl.BlockSpec((None, rows, LANES), lambda bi, h, c: (bi, c, 0)),
            pl.BlockSpec((1, dv), lambda bi, h, c: (0, 0)),
        ],
        out_specs=pl.BlockSpec((None, rows, hp * dv), lambda bi, h, c: (bi, c, h)),
        out_shape=jax.ShapeDtypeStruct((b, s, heads * dv), BF16),
        scratch_shapes=[pltpu.VMEM((hp, dk, dv), F32)],
        compiler_params=_params("parallel", "parallel", "arbitrary"),
        name="gdn_core",
    )(proj, proj, proj, proj, bg, norm_gain.reshape(1, dv))


def _sb_mixer(x, gain, w_qkv, q_gain, k_gain):
    b, s, d = x.shape
    heads = w_qkv.shape[1] // (3 * SB_HEAD_DIM)
    qkv = _sb_proj(x.reshape(b * s, d), gain, w_qkv.astype(BF16), q_gain, k_gain)
    return _sb_attn(qkv.reshape(b, s, -1), heads=heads)


def _ret_mixer(x, gain, w_in, gn_gain):
    b, s, d = x.shape
    heads = RET_HEADS
    dv = gn_gain.shape[0]
    dk = dv // 2
    half = dk // 2
    inv = ROPE_BASE ** (-jnp.arange(half, dtype=F32) / half)
    ang = jnp.arange(s, dtype=F32)[:, None] * inv[None, :]
    log_gamma = jnp.log1p(-jnp.exp2(-5.0 - jnp.arange(heads, dtype=F32)))
    lg_rows = jnp.broadcast_to(log_gamma[:, None, None], (heads, 1, LANES))
    proj = _ret_proj(x.reshape(b * s, d), gain, w_in.astype(BF16), jnp.cos(ang), jnp.sin(ang),
                     seq=s, dk=dk, heads=heads)
    return _ret_core(proj.reshape(b, s, -1), lg_rows, gn_gain, heads=heads, dk=dk, dv=dv)


def _gdn_mixer(x, gain, w_in, conv_w, a_log, dt_bias, norm_gain):
    b, s, d = x.shape
    heads = a_log.shape[0]
    dk = GDN_K_DIM
    dv = norm_gain.shape[0]
    n_main = 2 * heads * dk + 2 * heads * dv
    w_ab = jnp.zeros((d, LANES), BF16).at[:, :2 * heads].set(w_in[:, n_main:].astype(BF16))
    pad = lambda p: jnp.zeros((1, LANES), F32).at[0, :heads].set(p)
    proj, bg = _gdn_proj(x.reshape(b * s, d), gain, w_in[:, :n_main].astype(BF16), w_ab, conv_w,
                         pad(a_log), pad(dt_bias), seq=s, heads=heads, dk=dk)
    return _gdn_core(proj.reshape(b, s, -1), bg.reshape(b, s, LANES), norm_gain, heads=heads, dk=dk, dv=dv)


def _ffn_layer(x, gain, w_in, w_out, *, o=None, w_o=None, pool=None):
    b, s, d = x.shape
    proj = None if o is None else (o.reshape(b * s, -1), w_o.astype(BF16))
    out = _ffn(x.reshape(b * s, d), gain, w_in.astype(BF16), w_out.astype(BF16), seq=s, proj=proj, pool=pool)
    return out.reshape(b, s, d)


def kernel(x, norm_mix, norm_ffn, ffn_w_in, ffn_w_out, pool_w, pool_scale, sb_w_qkv, sb_q_gain, sb_k_gain, sb_w_o, ret_w_in, ret_gn_gain, ret_w_o, gdn_w_in, gdn_conv_w, gdn_a_log, gdn_dt_bias, gdn_norm_gain, gdn_w_o):
    depth = norm_mix.shape[0]
    for layer in range(depth):
        mixer, occ = layer % 4, layer // 4
        ffn = functools.partial(_ffn_layer, x, norm_ffn[layer], ffn_w_in[layer], ffn_w_out[layer])
        if mixer == 0:
            x = ffn(pool=(norm_mix[layer], pool_w[occ].astype(BF16), pool_scale[occ]))
        elif mixer == 1:
            x = ffn(o=_sb_mixer(x, norm_mix[layer], sb_w_qkv[occ], sb_q_gain[occ], sb_k_gain[occ]), w_o=sb_w_o[occ])
        elif mixer == 2:
            x = ffn(o=_ret_mixer(x, norm_mix[layer], ret_w_in[occ], ret_gn_gain[occ]), w_o=ret_w_o[occ])
        else:
            x = ffn(o=_gdn_mixer(x, norm_mix[layer], gdn_w_in[occ], gdn_conv_w[occ], gdn_a_log[occ],
                                 gdn_dt_bias[occ], gdn_norm_gain[occ]), w_o=gdn_w_o[occ])
    return x
```
